```python
import jax
import jax.numpy as jnp
from jax import lax
import numpy as np


D_MODEL = 1024
BATCH = 4
SEQ = 8192
DEPTH = 1

N_HEADS = 16
HEAD_DIM = D_MODEL // N_HEADS
ATTN_WIDTH = N_HEADS * HEAD_DIM
CONV_CHANNELS = D_MODEL
CONV_WIDTH = 31
MOBA_BLOCK = 256
MOBA_TOPK = 3
Q_CHUNK = 32
D_FF = 2816
EPS = 1e-6
NEG_INF = -1e30
IN_COLS = 2 * CONV_CHANNELS + 3 * ATTN_WIDTH + 2 * D_MODEL
SPLITS = (
    CONV_CHANNELS,
    2 * CONV_CHANNELS,
    2 * CONV_CHANNELS + ATTN_WIDTH,
    2 * CONV_CHANNELS + 2 * ATTN_WIDTH,
    2 * CONV_CHANNELS + 3 * ATTN_WIDTH,
    2 * CONV_CHANNELS + 3 * ATTN_WIDTH + D_MODEL,
)

kernel_name = 'hybrid_conformer_conv_moba_block'


def rms_norm(x, g):
    xf = x.astype(jnp.float32)
    y = xf * lax.rsqrt(jnp.mean(xf * xf, axis=-1, keepdims=True) + EPS)
    return (y * g.astype(jnp.float32)).astype(x.dtype)


def layer_norm(x, g, b):
    xf = x.astype(jnp.float32)
    mu = jnp.mean(xf, axis=-1, keepdims=True)
    var = jnp.mean(jnp.square(xf - mu), axis=-1, keepdims=True)
    y = (xf - mu) * lax.rsqrt(var + EPS)
    return (y * g.astype(jnp.float32) + b.astype(jnp.float32)).astype(x.dtype)


def swiglu(h, w_gate, w_up, w_down):
    return (jax.nn.silu(h @ w_gate) * (h @ w_up)) @ w_down


def causal_depthwise_conv(u, w, b):
    y = lax.conv_general_dilated(
        u, w[:, None, :], window_strides=(1,), padding=[(CONV_WIDTH - 1, 0)],
        dimension_numbers=('NWC', 'WIO', 'NWC'), feature_group_count=u.shape[-1])
    return y + b


def moba_attention(q, k, v):
    b, s, h, d = q.shape
    s_pad = -(-s // MOBA_BLOCK) * MOBA_BLOCK
    nb = s_pad // MOBA_BLOCK
    topk = min(MOBA_TOPK, nb)
    pad = ((0, 0), (0, s_pad - s), (0, 0), (0, 0))
    q, k, v = [jnp.pad(t, pad).transpose(0, 2, 1, 3) for t in (q, k, v)]
    kb = k.reshape(b, h, nb, MOBA_BLOCK, d)
    vb = v.reshape(b, h, nb, MOBA_BLOCK, d)
    k_mean = jnp.mean(kb.astype(jnp.float32), axis=3)
    gate = jnp.einsum('bhsd,bhnd->bhsn', q.astype(jnp.float32), k_mean)
    q_blk = jnp.arange(s_pad) // MOBA_BLOCK
    fully_past = jnp.arange(nb)[None, :] < q_blk[:, None]
    gate = jnp.where(fully_past, gate, -jnp.inf)
    _, sel = lax.top_k(gate, topk)
    sel_valid = sel < q_blk[:, None]
    n_chunks = s_pad // Q_CHUNK
    scale = d ** -0.5
    bi = jnp.arange(b)[:, None, None, None]
    hi = jnp.arange(h)[None, :, None, None]

    def to_chunks(t):
        return jnp.moveaxis(t.reshape(b, h, n_chunks, Q_CHUNK, *t.shape[3:]), 2, 0)

    def step(args):
        c, q_c, sel_c, valid_c = args
        q0 = c * Q_CHUNK
        own = q0 // MOBA_BLOCK
        k_own = lax.dynamic_index_in_dim(kb, own, axis=2, keepdims=False)
        v_own = lax.dynamic_index_in_dim(vb, own, axis=2, keepdims=False)
        q_pos = q0 + jnp.arange(Q_CHUNK)
        k_pos = own * MOBA_BLOCK + jnp.arange(MOBA_BLOCK)
        s_own = jnp.einsum('bhqd,bhkd->bhqk', q_c, k_own).astype(jnp.float32) * scale
        s_own = jnp.where(k_pos[None, :] <= q_pos[:, None], s_own, NEG_INF)
        k_sel = kb[bi, hi, sel_c]
        v_sel = vb[bi, hi, sel_c]
        s_sel = jnp.einsum('bhqd,bhqnkd->bhqnk', q_c, k_sel).astype(jnp.float32) * scale
        s_sel = jnp.where(valid_c[..., None], s_sel, NEG_INF)
        logits = jnp.concatenate([s_own, s_sel.reshape(b, h, Q_CHUNK, topk * MOBA_BLOCK)], axis=-1)
        p = jax.nn.softmax(logits, axis=-1)
        p_own = p[..., :MOBA_BLOCK]
        p_sel = p[..., MOBA_BLOCK:].reshape(b, h, Q_CHUNK, topk, MOBA_BLOCK)
        o = (jnp.einsum('bhqk,bhkd->bhqd', p_own, v_own.astype(jnp.float32))
             + jnp.einsum('bhqnk,bhqnkd->bhqd', p_sel, v_sel.astype(jnp.float32)))
        return o.astype(q_c.dtype)

    out = lax.map(step, (jnp.arange(n_chunks), to_chunks(q), to_chunks(sel), to_chunks(sel_valid)))
    out = out.transpose(1, 0, 3, 2, 4).reshape(b, s_pad, h * d)[:, :s]
    return out


def hybrid_layer(x, ffn1_norm, ffn1_w_gate, ffn1_w_up, ffn1_w_down, mix_norm, w_in,
                 conv_dw, conv_dw_bias, conv_ln_gain, conv_ln_bias, w_conv_proj,
                 q_norm, k_norm, w_attn_proj, w_out,
                 ffn2_norm, ffn2_w_gate, ffn2_w_up, ffn2_w_down):
    b, s, _ = x.shape
    x = x + 0.5 * swiglu(rms_norm(x, ffn1_norm), ffn1_w_gate, ffn1_w_up, ffn1_w_down)
    h = rms_norm(x, mix_norm)
    proj = h @ w_in
    conv_a, conv_b, q, k, v, gate_a, gate_b = jnp.split(proj, SPLITS, axis=-1)
    u = conv_a * jax.nn.sigmoid(conv_b)
    u = causal_depthwise_conv(u, conv_dw, conv_dw_bias)
    u = jax.nn.silu(layer_norm(u, conv_ln_gain, conv_ln_bias))
    y_a = u @ w_conv_proj
    q = rms_norm(q.reshape(b, s, N_HEADS, HEAD_DIM), q_norm)
    k = rms_norm(k.reshape(b, s, N_HEADS, HEAD_DIM), k_norm)
    v = v.reshape(b, s, N_HEADS, HEAD_DIM)
    y_b = moba_attention(q, k, v) @ w_attn_proj
    mixed = jax.nn.sigmoid(gate_a) * y_a + jax.nn.sigmoid(gate_b) * y_b
    x = x + mixed @ w_out
    x = x + 0.5 * swiglu(rms_norm(x, ffn2_norm), ffn2_w_gate, ffn2_w_up, ffn2_w_down)
    return x


def setup_inputs(seed: int = 0) -> dict:
    key = jax.random.key(seed)
    ks = jax.random.split(key, 20)
    L = DEPTH
    f32 = jnp.float32

    def normal(k, shape, fan_in):
        return jax.random.normal(k, shape, f32) * (fan_in ** -0.5)

    def gain(k, n):
        return 1.0 + 0.02 * jax.random.normal(k, (L, n), f32)

    def bias(k, n):
        return 0.02 * jax.random.normal(k, (L, n), f32)

    return {
        'x': jax.random.normal(ks[0], (BATCH, SEQ, D_MODEL), f32),
        'ffn1_norm': gain(ks[1], D_MODEL),
        'ffn1_w_gate': normal(ks[2], (L, D_MODEL, D_FF), D_MODEL),
        'ffn1_w_up': normal(ks[3], (L, D_MODEL, D_FF), D_MODEL),
        'ffn1_w_down': normal(ks[4], (L, D_FF, D_MODEL), D_FF),
        'mix_norm': gain(ks[5], D_MODEL),
        'w_in': normal(ks[6], (L, D_MODEL, IN_COLS), D_MODEL),
        'conv_dw': normal(ks[7], (L, CONV_WIDTH, CONV_CHANNELS), CONV_WIDTH),
        'conv_dw_bias': bias(ks[8], CONV_CHANNELS),
        'conv_ln_gain': gain(ks[9], CONV_CHANNELS),
        'conv_ln_bias': bias(ks[10], CONV_CHANNELS),
        'w_conv_proj': normal(ks[11], (L, CONV_CHANNELS, D_MODEL), CONV_CHANNELS),
        'q_norm': gain(ks[12], HEAD_DIM),
        'k_norm': gain(ks[13], HEAD_DIM),
        'w_attn_proj': normal(ks[14], (L, ATTN_WIDTH, D_MODEL), ATTN_WIDTH),
        'w_out': normal(ks[15], (L, D_MODEL, D_MODEL), D_MODEL),
        'ffn2_norm': gain(ks[16], D_MODEL),
        'ffn2_w_gate': normal(ks[17], (L, D_MODEL, D_FF), D_MODEL),
        'ffn2_w_up': normal(ks[18], (L, D_MODEL, D_FF), D_MODEL),
        'ffn2_w_down': normal(ks[19], (L, D_FF, D_MODEL), D_FF),
    }


def reference(x, ffn1_norm, ffn1_w_gate, ffn1_w_up, ffn1_w_down, mix_norm, w_in,
              conv_dw, conv_dw_bias, conv_ln_gain, conv_ln_bias, w_conv_proj,
              q_norm, k_norm, w_attn_proj, w_out,
              ffn2_norm, ffn2_w_gate, ffn2_w_up, ffn2_w_down):
    for i in range(DEPTH):
        x = hybrid_layer(
            x, ffn1_norm[i], ffn1_w_gate[i], ffn1_w_up[i], ffn1_w_down[i], mix_norm[i], w_in[i],
            conv_dw[i], conv_dw_bias[i], conv_ln_gain[i], conv_ln_bias[i], w_conv_proj[i],
            q_norm[i], k_norm[i], w_attn_proj[i], w_out[i],
            ffn2_norm[i], ffn2_w_gate[i], ffn2_w_up[i], ffn2_w_down[i])
    return x
```

```python
import functools

import jax
import jax.numpy as jnp
from jax import lax
from jax.experimental import pallas as pl
from jax.experimental.pallas import tpu as pltpu

F32 = jnp.float32
BF16 = jnp.bfloat16

N_HEADS = 16
HEAD_DIM = 64
CONV_WIDTH = 31
MOBA_BLOCK = 256
MOBA_TOPK = 3
EPS = 1e-6
NEG_INF = -1e30

LANES = 128
SUBLANES = 8
HALO = 32
TOKEN_TILE = 512
FF_CHUNK = 256
CONV_ROWS = 16
VMEM_LIMIT = 56 * 1024 * 1024


def _params(*sem):
    return pltpu.CompilerParams(dimension_semantics=sem, vmem_limit_bytes=VMEM_LIMIT)


def _const_spec(shape):
    nd = len(shape)
    return pl.BlockSpec(shape, lambda *_: (0,) * nd, pipeline_mode=pl.Buffered(1))


def _rms_rows(x, gain):
    ms = jnp.mean(x * x, axis=-1, keepdims=True)
    return x * lax.rsqrt(ms + EPS) * gain


def _dot(a, b):
    return jnp.dot(a, b, preferred_element_type=F32)


def _ffn_kernel(x_ref, g_ref, wg_ref, wu_ref, wd_ref, o_ref):
    x = x_ref[...]
    h = _rms_rows(x, g_ref[...]).astype(BF16)

    def body(c, acc):
        g = _dot(h, wg_ref[c])
        u = _dot(h, wu_ref[c])
        a = (g * jax.nn.sigmoid(g) * u).astype(BF16)
        return acc + _dot(a, wd_ref[c])

    acc = lax.fori_loop(0, wg_ref.shape[0], body, jnp.zeros(x.shape, F32))
    o_ref[...] = x + 0.5 * acc


def _ffn(x2d, gain, w_gate, w_up, w_down):
    n, d = x2d.shape
    f = w_gate.shape[1]
    nc = f // FF_CHUNK
    wg = w_gate.astype(BF16).reshape(d, nc, FF_CHUNK).transpose(1, 0, 2)
    wu = w_up.astype(BF16).reshape(d, nc, FF_CHUNK).transpose(1, 0, 2)
    wd = w_down.astype(BF16).reshape(nc, FF_CHUNK, d)
    row = pl.BlockSpec((TOKEN_TILE, d), lambda i: (i, 0))
    return pl.pallas_call(
        _ffn_kernel,
        grid=(n // TOKEN_TILE,),
        in_specs=[row, _const_spec((1, d)), _const_spec(wg.shape), _const_spec(wu.shape),
                  _const_spec(wd.shape)],
        out_specs=row,
        out_shape=jax.ShapeDtypeStruct((n, d), F32),
        compiler_params=_params("parallel"),
        name="ffn",
    )(x2d, gain.reshape(1, d), wg, wu, wd)


def _mix_in_kernel(x_ref, g_ref, w_ref, hs_ref, qg_ref, kg_ref, u_ref, q_ref, k_ref, v_ref):
    h = _rms_rows(x_ref[...], g_ref[...]).astype(BF16)
    a = _dot(h, w_ref[0])
    b = _dot(h, w_ref[1])
    u_ref[...] = (a * jax.nn.sigmoid(b)).astype(BF16)

    def head_norm(t, gain):
        ms = _dot((t * t).astype(BF16), hs_ref[...]) * (1.0 / HEAD_DIM)
        return t * lax.rsqrt(ms + EPS) * gain

    q = head_norm(_dot(h, w_ref[2]), qg_ref[...])
    q_ref[...] = (q * (HEAD_DIM ** -0.5)).astype(BF16)
    k_ref[...] = head_norm(_dot(h, w_ref[3]), kg_ref[...]).astype(BF16)
    v_ref[...] = _dot(h, w_ref[4]).astype(BF16)


def _mix_in(x2d, gain, w5, q_gain, k_gain):
    n, d = x2d.shape
    lane_head = jnp.arange(d) // HEAD_DIM
    head_sum = (lane_head[:, None] == lane_head[None, :]).astype(BF16)
    row = pl.BlockSpec((TOKEN_TILE, d), lambda i: (i, 0))
    out = jax.ShapeDtypeStruct((n, d), BF16)
    return pl.pallas_call(
        _mix_in_kernel,
        grid=(n // TOKEN_TILE,),
        in_specs=[row, _const_spec((1, d)), _const_spec(w5.shape), _const_spec((d, d)),
                  _const_spec((1, d)), _const_spec((1, d))],
        out_specs=[row, row, row, row],
        out_shape=[out, out, out, out],
        compiler_params=_params("parallel"),
        name="mix_in",
    )(x2d, gain.reshape(1, d), w5, head_sum,
      jnp.tile(q_gain, N_HEADS).reshape(1, d), jnp.tile(k_gain, N_HEADS).reshape(1, d))


def _conv_kernel(u_ref, halo_ref, w_ref, b_ref, lg_ref, lb_ref, o_ref, sh_ref):
    rows = u_ref.shape[1]
    total = HALO + rows
    first = pl.program_id(1) == 0
    halo = halo_ref[0].astype(F32)
    sh_ref[0, 0:HALO, :] = jnp.where(first, jnp.zeros_like(halo), halo)
    sh_ref[0, HALO:total, :] = u_ref[0].astype(F32)
    for r in range(1, SUBLANES):
        sh_ref[r, 0:total - r, :] = sh_ref[0, r:total, :]
    lead = HALO - (CONV_WIDTH - 1)

    def body(r, carry):
        base = pl.multiple_of(r * CONV_ROWS, CONV_ROWS)
        acc = jnp.broadcast_to(b_ref[...], (CONV_ROWS, u_ref.shape[2]))
        for j in range(CONV_WIDTH):
            shift = (lead + j) % SUBLANES
            tap = sh_ref[shift, pl.ds(base + (lead + j - shift), CONV_ROWS), :]
            acc = acc + tap * w_ref[j:j + 1, :]
        mu = jnp.mean(acc, axis=-1, keepdims=True)
        cen = acc - mu
        var = jnp.mean(cen * cen, axis=-1, keepdims=True)
        y = cen * lax.rsqrt(var + EPS) * lg_ref[...] + lb_ref[...]
        o_ref[0, pl.ds(base, CONV_ROWS), :] = (y * jax.nn.sigmoid(y)).astype(BF16)
        return carry

    lax.fori_loop(0, rows // CONV_ROWS, body, 0)


def _conv(u3d, w, bias, ln_gain, ln_bias):
    b, s, c = u3d.shape
    per = TOKEN_TILE // HALO
    tile = pl.BlockSpec((1, TOKEN_TILE, c), lambda bi, i: (bi, i, 0))
    halo = pl.BlockSpec((1, HALO, c), lambda bi, i: (bi, jnp.maximum(i * per - 1, 0), 0))
    return pl.pallas_call(
        _conv_kernel,
        grid=(b, s // TOKEN_TILE),
        in_specs=[tile, halo, _const_spec((CONV_WIDTH, c)), _const_spec((1, c)),
                  _const_spec((1, c)), _const_spec((1, c))],
        out_specs=tile,
        out_shape=jax.ShapeDtypeStruct((b, s, c), BF16),
        scratch_shapes=[pltpu.VMEM((SUBLANES, HALO + TOKEN_TILE, c), F32)],
        compiler_params=_params("parallel", "parallel"),
        name="conv",
    )(u3d, u3d, w, bias.reshape(1, c), ln_gain.reshape(1, c), ln_bias.reshape(1, c))


def _attn_kernel(q_ref, k_ref, v_ref, o_ref, kaug_ref, kmean_ref):
    i = pl.program_id(2)
    s_len = k_ref.shape[1]
    nb = s_len // MOBA_BLOCK
    blk = MOBA_BLOCK

    @pl.when(i == 0)
    def _():
        k = k_ref[0]
        kaug_ref[:, 0:LANES] = k
        row_blk = lax.broadcasted_iota(jnp.int32, (s_len, LANES), 0) // blk
        lane = lax.broadcasted_iota(jnp.int32, (s_len, LANES), 1)
        kaug_ref[:, LANES:2 * LANES] = (row_blk == lane).astype(BF16)
        kmean_ref[...] = jnp.zeros_like(kmean_ref)
        kmean_ref[0:nb, :] = jnp.mean(k.astype(F32).reshape(nb, blk, LANES), axis=1)

    qp = q_ref[0]
    lane_q = lax.broadcasted_iota(jnp.int32, (blk, LANES), 1)
    blk_row = lax.broadcasted_iota(jnp.int32, (LANES, blk), 0)
    q_pos = lax.broadcasted_iota(jnp.int32, (blk, blk), 0)
    k_pos = lax.broadcasted_iota(jnp.int32, (blk, blk), 1)
    kmean = kmean_ref[...]

    outs = []
    for e in range(2):
        in_head = (lane_q // HEAD_DIM) == e
        qe = jnp.where(in_head, qp, jnp.zeros_like(qp))
        gate = lax.dot_general(kmean, qe.astype(F32), (((1,), (1,)), ((), ())),
                               preferred_element_type=F32)
        gate = jnp.where(blk_row < i, gate, -jnp.inf)
        sel = blk_row == i
        for _ in range(MOBA_TOPK):
            mx = jnp.max(gate, axis=0, keepdims=True)
            first = jnp.min(jnp.where(gate == mx, blk_row, LANES), axis=0, keepdims=True)
            pick = (blk_row == first) & (mx > -jnp.inf)
            sel = sel | pick
            gate = jnp.where(pick, -jnp.inf, gate)
        bias = jnp.where(sel, 0.0, NEG_INF).astype(F32).T.astype(BF16)
        q_aug = jnp.concatenate([qe, bias], axis=1)

        def scores(j):
            kj = kaug_ref[pl.ds(pl.multiple_of(j * blk, blk), blk), :]
            return lax.dot_general(q_aug, kj, (((1,), (1,)), ((), ())),
                                   preferred_element_type=F32)

        def pv(p, j):
            vj = v_ref[0, pl.ds(pl.multiple_of(j * blk, blk), blk), :]
            return _dot(p.astype(BF16), vj)

        s = jnp.where(k_pos <= q_pos, scores(i), NEG_INF)
        m = jnp.max(s, axis=1, keepdims=True)
        p = jnp.exp(s - m)
        l = jnp.sum(p, axis=1, keepdims=True)
        acc = pv(p, i)

        def body(j, carry):
            m, l, acc = carry
            s = scores(j)
            m_new = jnp.maximum(m, jnp.max(s, axis=1, keepdims=True))
            alpha = jnp.exp(m - m_new)
            p = jnp.exp(s - m_new)
            l = alpha * l + jnp.sum(p, axis=1, keepdims=True)
            acc = alpha * acc + pv(p, j)
            return m_new, l, acc

        m, l, acc = lax.fori_loop(0, i, body, (m, l, acc))
        outs.append(acc / l)

    o_ref[0] = jnp.where(lane_q < HEAD_DIM, outs[0], outs[1]).astype(BF16)


def _attn(q, k, v):
    b, s, d = q.shape
    pairs = d // LANES
    nb = s // MOBA_BLOCK
    qspec = pl.BlockSpec((1, MOBA_BLOCK, LANES), lambda bi, hp, i: (bi, i, hp))
    kvspec = pl.BlockSpec((1, s, LANES), lambda bi, hp, i: (bi, 0, hp))
    return pl.pallas_call(
        _attn_kernel,
        grid=(b, pairs, nb),
        in_specs=[qspec, kvspec, kvspec],
        out_specs=qspec,
        out_shape=jax.ShapeDtypeStruct((b, s, d), BF16),
        scratch_shapes=[pltpu.VMEM((s, 2 * LANES), BF16), pltpu.VMEM((LANES, LANES), F32)],
        compiler_params=_params("parallel", "parallel", "arbitrary"),
        name="moba_attn",
    )(q, k, v)


def _merge_kernel(x_ref, uc_ref, at_ref, g_ref, wga_ref, wgb_ref, wcp_ref, wap_ref, wo_ref, o_ref):
    x = x_ref[...]
    h = _rms_rows(x, g_ref[...]).astype(BF16)
    ya = jax.nn.sigmoid(_dot(h, wga_ref[...])) * _dot(uc_ref[...], wcp_ref[...])
    yb = jax.nn.sigmoid(_dot(h, wgb_ref[...])) * _dot(at_ref[...], wap_ref[...])
    o_ref[...] = x + _dot((ya + yb).astype(BF16), wo_ref[...])


def _merge(x2d, uc, at, gain, w_ga, w_gb, w_cp, w_ap, w_o):
    n, d = x2d.shape
    row = pl.BlockSpec((TOKEN_TILE, d), lambda i: (i, 0))
    wspec = _const_spec((d, d))
    return pl.pallas_call(
        _merge_kernel,
        grid=(n // TOKEN_TILE,),
        in_specs=[row, row, row, _const_spec((1, d)), wspec, wspec, wspec, wspec, wspec],
        out_specs=row,
        out_shape=jax.ShapeDtypeStruct((n, d), F32),
        compiler_params=_params("parallel"),
        name="merge",
    )(x2d, uc, at, gain.reshape(1, d), w_ga, w_gb, w_cp, w_ap, w_o)


def _layer(x, ffn1_norm, ffn1_w_gate, ffn1_w_up, ffn1_w_down, mix_norm, w_in,
           conv_dw, conv_dw_bias, conv_ln_gain, conv_ln_bias, w_conv_proj,
           q_norm, k_norm, w_attn_proj, w_out,
           ffn2_norm, ffn2_w_gate, ffn2_w_up, ffn2_w_down):
    b, s, d = x.shape
    n = b * s
    assert d == N_HEADS * HEAD_DIM and s % TOKEN_TILE == 0 and s % MOBA_BLOCK == 0
    x1 = _ffn(x.reshape(n, d), ffn1_norm, ffn1_w_gate, ffn1_w_up, ffn1_w_down)
    w_in_b = w_in.astype(BF16)
    w5 = w_in_b[:, :5 * d].reshape(d, 5, d).transpose(1, 0, 2)
    u, q, k, v = _mix_in(x1, mix_norm, w5, q_norm, k_norm)
    uc = _conv(u.reshape(b, s, d), conv_dw, conv_dw_bias, conv_ln_gain, conv_ln_bias)
    at = _attn(q.reshape(b, s, d), k.reshape(b, s, d), v.reshape(b, s, d))
    x2 = _merge(x1, uc.reshape(n, d), at.reshape(n, d), mix_norm,
                w_in_b[:, 5 * d:6 * d], w_in_b[:, 6 * d:7 * d],
                w_conv_proj.astype(BF16), w_attn_proj.astype(BF16), w_out.astype(BF16))
    x3 = _ffn(x2, ffn2_norm, ffn2_w_gate, ffn2_w_up, ffn2_w_down)
    return x3.reshape(b, s, d)


def kernel(x, ffn1_norm, ffn1_w_gate, ffn1_w_up, ffn1_w_down, mix_norm, w_in, conv_dw, conv_dw_bias, conv_ln_gain, conv_ln_bias, w_conv_proj, q_norm, k_norm, w_attn_proj, w_out, ffn2_norm, ffn2_w_gate, ffn2_w_up, ffn2_w_down):
    for i in range(ffn1_norm.shape[0]):
        x = _layer(
            x, ffn1_norm[i], ffn1_w_gate[i], ffn1_w_up[i], ffn1_w_down[i], mix_norm[i], w_in[i],
            conv_dw[i], conv_dw_bias[i], conv_ln_gain[i], conv_ln_bias[i], w_conv_proj[i],
            q_norm[i], k_norm[i], w_attn_proj[i], w_out[i],
            ffn2_norm[i], ffn2_w_gate[i], ffn2_w_up[i], ffn2_w_down[i])
    return x
```

```python
import math

import jax
import jax.numpy as jnp
from jax import lax
from jax.experimental import pallas as pl
from jax.experimental.pallas import tpu as pltpu

F32 = jnp.float32
BF16 = jnp.bfloat16

N_HEADS = 16
HEAD_DIM = 64
CONV_WIDTH = 31
MOBA_BLOCK = 256
MOBA_TOPK = 3
EPS = 1e-6
NEG_INF = -1e30
Q_SCALE = HEAD_DIM ** -0.5 * math.log2(math.e)

LANES = 128
SUBLANES = 8
HALO = 32
TOKEN_TILE = 512
FF_CHUNK = 256
CONV_ROWS = 32
ATTN_QBLOCKS = 2
VMEM_LIMIT = 56 * 1024 * 1024


def _params(*sem):
    return pltpu.CompilerParams(dimension_semantics=sem, vmem_limit_bytes=VMEM_LIMIT)


def _const_spec(shape):
    nd = len(shape)
    return pl.BlockSpec(shape, lambda *_: (0,) * nd, pipeline_mode=pl.Buffered(1))


def _rms_rows(x, gain):
    ms = jnp.mean(x * x, axis=-1, keepdims=True)
    return x * lax.rsqrt(ms + EPS) * gain


def _dot(a, b):
    return jnp.dot(a, b, preferred_element_type=F32)


def _ffn_kernel(x_ref, g_ref, wg_ref, wu_ref, wd_ref, o_ref):
    x = x_ref[...]
    h = _rms_rows(x, g_ref[...]).astype(BF16)

    def body(c, acc):
        g = _dot(h, wg_ref[c])
        u = _dot(h, wu_ref[c])
        a = (g * jax.nn.sigmoid(g) * u).astype(BF16)
        return acc + _dot(a, wd_ref[c])

    acc = lax.fori_loop(0, wg_ref.shape[0], body, jnp.zeros(x.shape, F32), unroll=True)
    o_ref[...] = x + 0.5 * acc


def _ffn(x2d, gain, w_gate, w_up, w_down):
    n, d = x2d.shape
    f = w_gate.shape[1]
    nc = f // FF_CHUNK
    wg = w_gate.astype(BF16).reshape(d, nc, FF_CHUNK).transpose(1, 0, 2)
    wu = w_up.astype(BF16).reshape(d, nc, FF_CHUNK).transpose(1, 0, 2)
    wd = w_down.astype(BF16).reshape(nc, FF_CHUNK, d)
    row = pl.BlockSpec((TOKEN_TILE, d), lambda i: (i, 0))
    return pl.pallas_call(
        _ffn_kernel,
        grid=(n // TOKEN_TILE,),
        in_specs=[row, _const_spec((1, d)), _const_spec(wg.shape), _const_spec(wu.shape),
                  _const_spec(wd.shape)],
        out_specs=row,
        out_shape=jax.ShapeDtypeStruct((n, d), F32),
        compiler_params=_params("parallel"),
        name="ffn",
    )(x2d, gain.reshape(1, d), wg, wu, wd)


def _mix_in_kernel(x_ref, g_ref, w_ref, hs_ref, qg_ref, kg_ref, u_ref, q_ref, k_ref, v_ref):
    h = _rms_rows(x_ref[...], g_ref[...]).astype(BF16)
    a = _dot(h, w_ref[0])
    b = _dot(h, w_ref[1])
    u_ref[...] = (a * jax.nn.sigmoid(b)).astype(BF16)

    def head_norm(t, gain):
        ms = _dot((t * t).astype(BF16), hs_ref[...]) * (1.0 / HEAD_DIM)
        return t * lax.rsqrt(ms + EPS) * gain

    q = head_norm(_dot(h, w_ref[2]), qg_ref[...])
    q_ref[...] = (q * Q_SCALE).astype(BF16)
    k_ref[...] = head_norm(_dot(h, w_ref[3]), kg_ref[...]).astype(BF16)
    v_ref[...] = _dot(h, w_ref[4]).astype(BF16)


def _mix_in(x2d, gain, w5, q_gain, k_gain):
    n, d = x2d.shape
    lane_head = jnp.arange(d) // HEAD_DIM
    head_sum = (lane_head[:, None] == lane_head[None, :]).astype(BF16)
    row = pl.BlockSpec((TOKEN_TILE, d), lambda i: (i, 0))
    out = jax.ShapeDtypeStruct((n, d), BF16)
    return pl.pallas_call(
        _mix_in_kernel,
        grid=(n // TOKEN_TILE,),
        in_specs=[row, _const_spec((1, d)), _const_spec(w5.shape), _const_spec((d, d)),
                  _const_spec((1, d)), _const_spec((1, d))],
        out_specs=[row, row, row, row],
        out_shape=[out, out, out, out],
        compiler_params=_params("parallel"),
        name="mix_in",
    )(x2d, gain.reshape(1, d), w5, head_sum,
      jnp.tile(q_gain, N_HEADS).reshape(1, d), jnp.tile(k_gain, N_HEADS).reshape(1, d))


def _conv_kernel(u_ref, halo_ref, w_ref, b_ref, lg_ref, lb_ref, o_ref, sh_ref, conv_ref):
    rows = u_ref.shape[1]
    total = HALO + rows
    first = pl.program_id(1) == 0
    halo = halo_ref[0].astype(F32)
    sh_ref[0, 0:HALO, :] = jnp.where(first, jnp.zeros_like(halo), halo)
    sh_ref[0, HALO:total, :] = u_ref[0].astype(F32)
    for r in range(1, SUBLANES):
        sh_ref[r, 0:total - r, :] = sh_ref[0, r:total, :]
    lead = HALO - (CONV_WIDTH - 1)

    groups = CONV_ROWS // SUBLANES

    def conv_strip(base, lanes):
        accs = [b_ref[:, lanes]] * groups
        for shift in range(SUBLANES):
            taps = range((shift - lead) % SUBLANES, CONV_WIDTH, SUBLANES)
            ws = {j: w_ref[j, :, lanes] for j in taps}
            offs = sorted({lead + j - shift + g * SUBLANES for j in taps for g in range(groups)})
            for off in offs:
                rows8 = sh_ref[shift, pl.ds(base + off, SUBLANES), lanes]
                for j in taps:
                    g, rem = divmod(off - (lead + j - shift), SUBLANES)
                    if rem == 0 and 0 <= g < groups:
                        accs[g] = accs[g] + rows8 * ws[j]
        return accs

    def body(r, carry):
        base = pl.multiple_of(r * CONV_ROWS, CONV_ROWS)
        strips = [conv_strip(base, slice(t * LANES, (t + 1) * LANES))
                  for t in range(u_ref.shape[2] // LANES)]
        conv = [jnp.concatenate(parts, axis=1) for parts in zip(*strips)]
        conv_ref[pl.ds(base, CONV_ROWS), :] = jnp.concatenate(conv, axis=0)
        return carry

    lax.fori_loop(0, rows // CONV_ROWS, body, 0)
    acc = conv_ref[...]
    mu = jnp.mean(acc, axis=-1, keepdims=True)
    cen = acc - mu
    var = jnp.mean(cen * cen, axis=-1, keepdims=True)
    y = cen * lax.rsqrt(var + EPS) * lg_ref[0:1, :] + lb_ref[0:1, :]
    o_ref[0] = (y * jax.nn.sigmoid(y)).astype(BF16)


def _conv(u3d, w, bias, ln_gain, ln_bias):
    b, s, c = u3d.shape
    per = TOKEN_TILE // HALO
    tile = pl.BlockSpec((1, TOKEN_TILE, c), lambda bi, i: (bi, i, 0))
    halo = pl.BlockSpec((1, HALO, c), lambda bi, i: (bi, jnp.maximum(i * per - 1, 0), 0))
    rep = lambda p: jnp.broadcast_to(p.reshape(-1, 1, c), (p.size // c, SUBLANES, c))
    vec = _const_spec((SUBLANES, c))
    return pl.pallas_call(
        _conv_kernel,
        grid=(b, s // TOKEN_TILE),
        in_specs=[tile, halo, _const_spec((CONV_WIDTH, SUBLANES, c)), vec, vec, vec],
        out_specs=tile,
        out_shape=jax.ShapeDtypeStruct((b, s, c), BF16),
        scratch_shapes=[pltpu.VMEM((SUBLANES, HALO + TOKEN_TILE, c), F32),
                        pltpu.VMEM((TOKEN_TILE, c), F32)],
        compiler_params=_params("parallel", "parallel"),
        name="conv",
    )(u3d, u3d, rep(w), rep(bias)[0], rep(ln_gain)[0], rep(ln_bias)[0])


def _attn_kernel(q_ref, k_ref, v_ref, o_ref, kaug_ref, kmean_ref):
    a = pl.program_id(2)
    s_len = k_ref.shape[1]
    nb = s_len // MOBA_BLOCK
    blk = MOBA_BLOCK
    span = ATTN_QBLOCKS * blk
    rows = 2 * span

    @pl.when(a == 0)
    def _():
        k = k_ref[0]
        kaug_ref[:, 0:LANES] = k
        row_blk = lax.broadcasted_iota(jnp.int32, (s_len, LANES), 0) // blk
        lane = lax.broadcasted_iota(jnp.int32, (s_len, LANES), 1)
        kaug_ref[:, LANES:2 * LANES] = (row_blk == lane).astype(BF16)
        kmean_ref[...] = jnp.mean(k.astype(F32).reshape(nb, blk, LANES), axis=1).astype(BF16)

    nt = (((1,), (1,)), ((), ()))
    qp = q_ref[0]
    q2 = jnp.concatenate([qp, qp], axis=0)
    row_head = lax.broadcasted_iota(jnp.int32, (rows, LANES), 0) // span
    lane_head = lax.broadcasted_iota(jnp.int32, (rows, LANES), 1) // HEAD_DIM
    qe = jnp.where(row_head == lane_head, q2, jnp.zeros_like(q2))

    cand = lax.broadcasted_iota(jnp.int32, (nb, rows), 0)
    q_blk = a * ATTN_QBLOCKS + (lax.broadcasted_iota(jnp.int32, (nb, rows), 1) % span) // blk
    gate = lax.dot_general(kmean_ref[...], qe, nt, preferred_element_type=F32)
    gate = jnp.where(cand < q_blk, gate, -jnp.inf)
    sel = cand == q_blk
    for _ in range(MOBA_TOPK):
        mx = jnp.max(gate, axis=0, keepdims=True)
        first = jnp.min(jnp.where(gate == mx, cand, nb), axis=0, keepdims=True)
        pick = (cand == first) & (mx > -jnp.inf)
        sel = sel | pick
        gate = jnp.where(pick, -jnp.inf, gate)
    bias_t = jnp.where(sel, 0.0, NEG_INF).astype(F32)
    bias_t = jnp.concatenate([bias_t, jnp.zeros((LANES - nb, rows), F32)], axis=0)
    q_aug = jnp.concatenate([qe, bias_t.T.astype(BF16)], axis=1)

    def scores(g):
        start = pl.multiple_of(g * span, span)
        return lax.dot_general(q_aug, kaug_ref[pl.ds(start, span), :], nt,
                               preferred_element_type=F32)

    def weighted_values(p, g):
        start = pl.multiple_of(g * span, span)
        return _dot(p.astype(BF16), v_ref[0, pl.ds(start, span), :])

    q_pos = lax.broadcasted_iota(jnp.int32, (rows, span), 0) % span
    k_pos = lax.broadcasted_iota(jnp.int32, (rows, span), 1)
    s = jnp.where(k_pos <= q_pos, scores(a), NEG_INF)
    m = jnp.max(s, axis=1, keepdims=True)
    p = jnp.exp2(s - m)
    l = jnp.sum(p, axis=1, keepdims=True)
    acc = weighted_values(p, a)

    def body(g, carry):
        m, l, acc = carry
        s = scores(g)
        m_new = jnp.maximum(m, jnp.max(s, axis=1, keepdims=True))
        alpha = jnp.exp2(m - m_new)
        p = jnp.exp2(s - m_new)
        l = alpha * l + jnp.sum(p, axis=1, keepdims=True)
        acc = alpha * acc + weighted_values(p, g)
        return m_new, l, acc

    m, l, acc = lax.fori_loop(0, a, body, (m, l, acc))
    out = acc / l
    lane_q = lax.broadcasted_iota(jnp.int32, (span, LANES), 1)
    o_ref[0] = jnp.where(lane_q < HEAD_DIM, out[0:span], out[span:rows]).astype(BF16)


def _attn(q, k, v):
    b, s, d = q.shape
    pairs = d // LANES
    nb = s // MOBA_BLOCK
    span = ATTN_QBLOCKS * MOBA_BLOCK
    qspec = pl.BlockSpec((1, span, LANES), lambda bi, hp, i: (bi, i, hp))
    kvspec = pl.BlockSpec((1, s, LANES), lambda bi, hp, i: (bi, 0, hp))
    return pl.pallas_call(
        _attn_kernel,
        grid=(b, pairs, s // span),
        in_specs=[qspec, kvspec, kvspec],
        out_specs=qspec,
        out_shape=jax.ShapeDtypeStruct((b, s, d), BF16),
        scratch_shapes=[pltpu.VMEM((s, 2 * LANES), BF16), pltpu.VMEM((nb, LANES), BF16)],
        compiler_params=_params("parallel", "parallel", "arbitrary"),
        name="moba_attn",
    )(q, k, v)


def _merge_kernel(x_ref, uc_ref, at_ref, g_ref, wga_ref, wgb_ref, wcp_ref, wap_ref, wo_ref, o_ref):
    x = x_ref[...]
    h = _rms_rows(x, g_ref[...]).astype(BF16)
    ya = jax.nn.sigmoid(_dot(h, wga_ref[...])) * _dot(uc_ref[...], wcp_ref[...])
    yb = jax.nn.sigmoid(_dot(h, wgb_ref[...])) * _dot(at_ref[...], wap_ref[...])
    o_ref[...] = x + _dot((ya + yb).astype(BF16), wo_ref[...])


def _merge(x2d, uc, at, gain, w_ga, w_gb, w_cp, w_ap, w_o):
    n, d = x2d.shape
    row = pl.BlockSpec((TOKEN_TILE, d), lambda i: (i, 0))
    wspec = _const_spec((d, d))
    return pl.pallas_call(
        _merge_kernel,
        grid=(n // TOKEN_TILE,),
        in_specs=[row, row, row, _const_spec((1, d)), wspec, wspec, wspec, wspec, wspec],
        out_specs=row,
        out_shape=jax.ShapeDtypeStruct((n, d), F32),
        compiler_params=_params("parallel"),
        name="merge",
    )(x2d, uc, at, gain.reshape(1, d), w_ga, w_gb, w_cp, w_ap, w_o)


def _layer(x, ffn1_norm, ffn1_w_gate, ffn1_w_up, ffn1_w_down, mix_norm, w_in,
           conv_dw, conv_dw_bias, conv_ln_gain, conv_ln_bias, w_conv_proj,
           q_norm, k_norm, w_attn_proj, w_out,
           ffn2_norm, ffn2_w_gate, ffn2_w_up, ffn2_w_down):
    b, s, d = x.shape
    n = b * s
    assert d == N_HEADS * HEAD_DIM and s % TOKEN_TILE == 0
    assert s % (ATTN_QBLOCKS * MOBA_BLOCK) == 0 and s // MOBA_BLOCK <= LANES
    x1 = _ffn(x.reshape(n, d), ffn1_norm, ffn1_w_gate, ffn1_w_up, ffn1_w_down)
    w_in_b = w_in.astype(BF16)
    w5 = w_in_b[:, :5 * d].reshape(d, 5, d).transpose(1, 0, 2)
    u, q, k, v = _mix_in(x1, mix_norm, w5, q_norm, k_norm)
    uc = _conv(u.reshape(b, s, d), conv_dw, conv_dw_bias, conv_ln_gain, conv_ln_bias)
    at = _attn(q.reshape(b, s, d), k.reshape(b, s, d), v.reshape(b, s, d))
    x2 = _merge(x1, uc.reshape(n, d), at.reshape(n, d), mix_norm,
                w_in_b[:, 5 * d:6 * d], w_in_b[:, 6 * d:7 * d],
                w_conv_proj.astype(BF16), w_attn_proj.astype(BF16), w_out.astype(BF16))
    x3 = _ffn(x2, ffn2_norm, ffn2_w_gate, ffn2_w_up, ffn2_w_down)
    return x3.reshape(b, s, d)


def kernel(x, ffn1_norm, ffn1_w_gate, ffn1_w_up, ffn1_w_down, mix_norm, w_in, conv_dw, conv_dw_bias, conv_ln_gain, conv_ln_bias, w_conv_proj, q_norm, k_norm, w_attn_proj, w_out, ffn2_norm, ffn2_w_gate, ffn2_w_up, ffn2_w_down):
    for i in range(ffn1_norm.shape[0]):
        x = _layer(
            x, ffn1_norm[i], ffn1_w_gate[i], ffn1_w_up[i], ffn1_w_down[i], mix_norm[i], w_in[i],
            conv_dw[i], conv_dw_bias[i], conv_ln_gain[i], conv_ln_bias[i], w_conv_proj[i],
            q_norm[i], k_norm[i], w_attn_proj[i], w_out[i],
            ffn2_norm[i], ffn2_w_gate[i], ffn2_w_up[i], ffn2_w_down[i])
    return x
```

```python
import math

import jax
import jax.numpy as jnp
from jax import lax
from jax.experimental import pallas as pl
from jax.experimental.pallas import tpu as pltpu

F32 = jnp.float32
BF16 = jnp.bfloat16

N_HEADS = 16
HEAD_DIM = 64
CONV_WIDTH = 31
MOBA_BLOCK = 256
MOBA_TOPK = 3
EPS = 1e-6
NEG_INF = -1e30
Q_SCALE = HEAD_DIM ** -0.5 * math.log2(math.e)

LANES = 128
SUBLANES = 8
HALO = 32
TOKEN_TILE = 512
FF_CHUNK = 256
CONV_ROWS = 32
ATTN_QBLOCKS = 2
VMEM_LIMIT = 56 * 1024 * 1024


def _params(*sem):
    return pltpu.CompilerParams(dimension_semantics=sem, vmem_limit_bytes=VMEM_LIMIT)


def _const_spec(shape):
    nd = len(shape)
    return pl.BlockSpec(shape, lambda *_: (0,) * nd, pipeline_mode=pl.Buffered(1))


def _rms_rows(x, gain):
    ms = jnp.mean(x * x, axis=-1, keepdims=True)
    return x * lax.rsqrt(ms + EPS) * gain


def _dot(a, b):
    return jnp.dot(a, b, preferred_element_type=F32)


def _ffn_kernel(x_ref, g_ref, wg_ref, wu_ref, wd_ref, o_ref):
    x = x_ref[...]
    h = _rms_rows(x, g_ref[...]).astype(BF16)

    def body(c, acc):
        g = _dot(h, wg_ref[c])
        u = _dot(h, wu_ref[c])
        a = (g * jax.nn.sigmoid(g) * u).astype(BF16)
        return acc + _dot(a, wd_ref[c])

    acc = lax.fori_loop(0, wg_ref.shape[0], body, jnp.zeros(x.shape, F32), unroll=True)
    o_ref[...] = x + 0.5 * acc


def _ffn(x2d, gain, w_gate, w_up, w_down):
    n, d = x2d.shape
    f = w_gate.shape[1]
    nc = f // FF_CHUNK
    wg = w_gate.astype(BF16).reshape(d, nc, FF_CHUNK).transpose(1, 0, 2)
    wu = w_up.astype(BF16).reshape(d, nc, FF_CHUNK).transpose(1, 0, 2)
    wd = w_down.astype(BF16).reshape(nc, FF_CHUNK, d)
    row = pl.BlockSpec((TOKEN_TILE, d), lambda i: (i, 0))
    return pl.pallas_call(
        _ffn_kernel,
        grid=(n // TOKEN_TILE,),
        in_specs=[row, _const_spec((1, d)), _const_spec(wg.shape), _const_spec(wu.shape),
                  _const_spec(wd.shape)],
        out_specs=row,
        out_shape=jax.ShapeDtypeStruct((n, d), F32),
        compiler_params=_params("parallel"),
        name="ffn",
    )(x2d, gain.reshape(1, d), wg, wu, wd)


def _mix_in_kernel(x_ref, g_ref, w_ref, hs_ref, qg_ref, kg_ref, u_ref, q_ref, k_ref, v_ref):
    h = _rms_rows(x_ref[...], g_ref[...]).astype(BF16)
    a = _dot(h, w_ref[0])
    b = _dot(h, w_ref[1])
    u_ref[...] = (a * jax.nn.sigmoid(b)).astype(BF16)

    def head_norm(t, gain):
        ms = _dot((t * t).astype(BF16), hs_ref[...]) * (1.0 / HEAD_DIM)
        return t * lax.rsqrt(ms + EPS) * gain

    q = head_norm(_dot(h, w_ref[2]), qg_ref[...])
    q_ref[...] = (q * Q_SCALE).astype(BF16)
    k_ref[...] = head_norm(_dot(h, w_ref[3]), kg_ref[...]).astype(BF16)
    v_ref[...] = _dot(h, w_ref[4]).astype(BF16)


def _mix_in(x2d, gain, w5, q_gain, k_gain):
    n, d = x2d.shape
    lane_head = jnp.arange(d) // HEAD_DIM
    head_sum = (lane_head[:, None] == lane_head[None, :]).astype(BF16)
    row = pl.BlockSpec((TOKEN_TILE, d), lambda i: (i, 0))
    out = jax.ShapeDtypeStruct((n, d), BF16)
    return pl.pallas_call(
        _mix_in_kernel,
        grid=(n // TOKEN_TILE,),
        in_specs=[row, _const_spec((1, d)), _const_spec(w5.shape), _const_spec((d, d)),
                  _const_spec((1, d)), _const_spec((1, d))],
        out_specs=[row, row, row, row],
        out_shape=[out, out, out, out],
        compiler_params=_params("parallel"),
        name="mix_in",
    )(x2d, gain.reshape(1, d), w5, head_sum,
      jnp.tile(q_gain, N_HEADS).reshape(1, d), jnp.tile(k_gain, N_HEADS).reshape(1, d))


def _conv_kernel(u_ref, halo_ref, w_ref, b_ref, lg_ref, lb_ref, o_ref, sh_ref, conv_ref):
    rows = u_ref.shape[1]
    total = HALO + rows
    first = pl.program_id(1) == 0
    halo = halo_ref[0].astype(F32)
    sh_ref[0, 0:HALO, :] = jnp.where(first, jnp.zeros_like(halo), halo)
    sh_ref[0, HALO:total, :] = u_ref[0].astype(F32)
    for r in range(1, SUBLANES):
        sh_ref[r, 0:total - r, :] = sh_ref[0, r:total, :]
    lead = HALO - (CONV_WIDTH - 1)

    groups = CONV_ROWS // SUBLANES

    def conv_strip(base, lanes):
        accs = [b_ref[:, lanes]] * groups
        for shift in range(SUBLANES):
            taps = range((shift - lead) % SUBLANES, CONV_WIDTH, SUBLANES)
            ws = {j: w_ref[j, :, lanes] for j in taps}
            offs = sorted({lead + j - shift + g * SUBLANES for j in taps for g in range(groups)})
            for off in offs:
                rows8 = sh_ref[shift, pl.ds(base + off, SUBLANES), lanes]
                for j in taps:
                    g, rem = divmod(off - (lead + j - shift), SUBLANES)
                    if rem == 0 and 0 <= g < groups:
                        accs[g] = accs[g] + rows8 * ws[j]
        return accs

    def body(r, carry):
        base = pl.multiple_of(r * CONV_ROWS, CONV_ROWS)
        strips = [conv_strip(base, slice(t * LANES, (t + 1) * LANES))
                  for t in range(u_ref.shape[2] // LANES)]
        conv = [jnp.concatenate(parts, axis=1) for parts in zip(*strips)]
        conv_ref[pl.ds(base, CONV_ROWS), :] = jnp.concatenate(conv, axis=0)
        return carry

    lax.fori_loop(0, rows // CONV_ROWS, body, 0)
    acc = conv_ref[...]
    mu = jnp.mean(acc, axis=-1, keepdims=True)
    cen = acc - mu
    var = jnp.mean(cen * cen, axis=-1, keepdims=True)
    y = cen * lax.rsqrt(var + EPS) * lg_ref[0:1, :] + lb_ref[0:1, :]
    o_ref[0] = (y * jax.nn.sigmoid(y)).astype(BF16)


def _conv(u3d, w, bias, ln_gain, ln_bias):
    b, s, c = u3d.shape
    per = TOKEN_TILE // HALO
    tile = pl.BlockSpec((1, TOKEN_TILE, c), lambda bi, i: (bi, i, 0))
    halo = pl.BlockSpec((1, HALO, c), lambda bi, i: (bi, jnp.maximum(i * per - 1, 0), 0))
    rep = lambda p: jnp.broadcast_to(p.reshape(-1, 1, c), (p.size // c, SUBLANES, c))
    vec = _const_spec((SUBLANES, c))
    return pl.pallas_call(
        _conv_kernel,
        grid=(b, s // TOKEN_TILE),
        in_specs=[tile, halo, _const_spec((CONV_WIDTH, SUBLANES, c)), vec, vec, vec],
        out_specs=tile,
        out_shape=jax.ShapeDtypeStruct((b, s, c), BF16),
        scratch_shapes=[pltpu.VMEM((SUBLANES, HALO + TOKEN_TILE, c), F32),
                        pltpu.VMEM((TOKEN_TILE, c), F32)],
        compiler_params=_params("parallel", "parallel"),
        name="conv",
    )(u3d, u3d, rep(w), rep(bias)[0], rep(ln_gain)[0], rep(ln_bias)[0])


def _attn_kernel(q_ref, k_ref, v_ref, o_ref, kaug_ref, vaug_ref, kmean_ref, qaug_ref,
                 sa_ref, sb_ref, ra_ref, rb_ref, m_ref, acc_ref):
    a = pl.program_id(2)
    s_len = k_ref.shape[1]
    nb = s_len // MOBA_BLOCK
    blk = MOBA_BLOCK
    span = ATTN_QBLOCKS * blk
    rows = 2 * span

    @pl.when(a == 0)
    def _():
        k = k_ref[0]
        kaug_ref[:, 0:LANES] = k
        row_blk = lax.broadcasted_iota(jnp.int32, (s_len, LANES), 0) // blk
        lane = lax.broadcasted_iota(jnp.int32, (s_len, LANES), 1)
        kaug_ref[:, LANES:2 * LANES] = (row_blk == lane).astype(BF16)
        vaug_ref[:, 0:LANES] = v_ref[0]
        vaug_ref[:, LANES:2 * LANES] = jnp.ones((s_len, LANES), BF16)
        kmean_ref[...] = jnp.mean(k.astype(F32).reshape(nb, blk, LANES), axis=1).astype(BF16)

    nt = (((1,), (1,)), ((), ()))
    qp = q_ref[0]
    q2 = jnp.concatenate([qp, qp], axis=0)
    row_head = lax.broadcasted_iota(jnp.int32, (rows, LANES), 0) // span
    lane_head = lax.broadcasted_iota(jnp.int32, (rows, LANES), 1) // HEAD_DIM
    qe = jnp.where(row_head == lane_head, q2, jnp.zeros_like(q2))

    cand = lax.broadcasted_iota(jnp.int32, (nb, rows), 0)
    q_blk = a * ATTN_QBLOCKS + (lax.broadcasted_iota(jnp.int32, (nb, rows), 1) % span) // blk
    gate = lax.dot_general(kmean_ref[...], qe, nt, preferred_element_type=F32)
    gate = jnp.where(cand < q_blk, gate, -jnp.inf)
    sel = cand == q_blk
    for _ in range(MOBA_TOPK):
        mx = jnp.max(gate, axis=0, keepdims=True)
        first = jnp.min(jnp.where(gate == mx, cand, nb), axis=0, keepdims=True)
        pick = (cand == first) & (mx > -jnp.inf)
        sel = sel | pick
        gate = jnp.where(pick, -jnp.inf, gate)
    bias_t = jnp.where(sel, 0.0, NEG_INF).astype(F32)
    bias_t = jnp.concatenate([bias_t, jnp.zeros((LANES - nb, rows), F32)], axis=0)
    qaug_ref[...] = jnp.concatenate([qe, bias_t.T.astype(BF16)], axis=1)

    def issue_scores(n, s_ref, rmax_ref):
        start = pl.multiple_of(n * span, span)
        s = lax.dot_general(qaug_ref[...], kaug_ref[pl.ds(start, span), :], nt,
                            preferred_element_type=F32)
        s_ref[...] = s
        rmax_ref[...] = jnp.broadcast_to(jnp.max(s, axis=1, keepdims=True), rmax_ref.shape)

    def absorb(n, s_ref, rmax_ref, own):
        s = s_ref[...]
        if own:
            q_pos = lax.broadcasted_iota(jnp.int32, (rows, span), 0) % span
            k_pos = lax.broadcasted_iota(jnp.int32, (rows, span), 1)
            s = jnp.where(k_pos <= q_pos, s, NEG_INF)
            rmax = jnp.broadcast_to(jnp.max(s, axis=1, keepdims=True), m_ref.shape)
        else:
            rmax = rmax_ref[...]
        m_old = m_ref[...]
        m_new = jnp.maximum(m_old, rmax)
        alpha = jnp.exp2(m_old - m_new)
        p = jnp.exp2(s - jnp.concatenate([m_new] * (span // LANES), axis=1))
        start = pl.multiple_of(n * span, span)
        acc_ref[...] = (jnp.concatenate([alpha, alpha], axis=1) * acc_ref[...]
                        + _dot(p.astype(BF16), vaug_ref[pl.ds(start, span), :]))
        m_ref[...] = m_new

    m_ref[...] = jnp.full(m_ref.shape, 0.1 * NEG_INF, F32)
    acc_ref[...] = jnp.zeros_like(acc_ref)
    issue_scores(0, sa_ref, ra_ref)

    def pair(t, carry):
        issue_scores(2 * t + 1, sb_ref, rb_ref)
        absorb(2 * t, sa_ref, ra_ref, False)
        issue_scores(2 * t + 2, sa_ref, ra_ref)
        absorb(2 * t + 1, sb_ref, rb_ref, False)
        return carry

    lax.fori_loop(0, a // 2, pair, 0)

    @pl.when(a % 2 == 1)
    def _():
        issue_scores(a, sb_ref, rb_ref)
        absorb(a - 1, sa_ref, ra_ref, False)
        absorb(a, sb_ref, rb_ref, True)

    @pl.when(a % 2 == 0)
    def _():
        absorb(a, sa_ref, ra_ref, True)

    out = acc_ref[:, 0:LANES] / acc_ref[:, LANES:2 * LANES]
    lane_q = lax.broadcasted_iota(jnp.int32, (span, LANES), 1)
    o_ref[0] = jnp.where(lane_q < HEAD_DIM, out[0:span], out[span:rows]).astype(BF16)


def _attn(q, k, v):
    b, s, d = q.shape
    pairs = d // LANES
    nb = s // MOBA_BLOCK
    span = ATTN_QBLOCKS * MOBA_BLOCK
    rows = 2 * span
    qspec = pl.BlockSpec((1, span, LANES), lambda bi, hp, i: (bi, i, hp))
    kvspec = pl.BlockSpec((1, s, LANES), lambda bi, hp, i: (bi, 0, hp))
    return pl.pallas_call(
        _attn_kernel,
        grid=(b, pairs, s // span),
        in_specs=[qspec, kvspec, kvspec],
        out_specs=qspec,
        out_shape=jax.ShapeDtypeStruct((b, s, d), BF16),
        scratch_shapes=[pltpu.VMEM((s, 2 * LANES), BF16), pltpu.VMEM((s, 2 * LANES), BF16),
                        pltpu.VMEM((nb, LANES), BF16),
                        pltpu.VMEM((rows, 2 * LANES), BF16),
                        pltpu.VMEM((rows, span), F32), pltpu.VMEM((rows, span), F32),
                        pltpu.VMEM((rows, LANES), F32), pltpu.VMEM((rows, LANES), F32),
                        pltpu.VMEM((rows, LANES), F32), pltpu.VMEM((rows, 2 * LANES), F32)],
        compiler_params=_params("parallel", "parallel", "arbitrary"),
        name="moba_attn",
    )(q, k, v)


def _merge_kernel(x_ref, uc_ref, at_ref, g_ref, wga_ref, wgb_ref, wcp_ref, wap_ref, wo_ref, o_ref):
    x = x_ref[...]
    h = _rms_rows(x, g_ref[...]).astype(BF16)
    ya = jax.nn.sigmoid(_dot(h, wga_ref[...])) * _dot(uc_ref[...], wcp_ref[...])
    yb = jax.nn.sigmoid(_dot(h, wgb_ref[...])) * _dot(at_ref[...], wap_ref[...])
    o_ref[...] = x + _dot((ya + yb).astype(BF16), wo_ref[...])


def _merge(x2d, uc, at, gain, w_ga, w_gb, w_cp, w_ap, w_o):
    n, d = x2d.shape
    row = pl.BlockSpec((TOKEN_TILE, d), lambda i: (i, 0))
    wspec = _const_spec((d, d))
    return pl.pallas_call(
        _merge_kernel,
        grid=(n // TOKEN_TILE,),
        in_specs=[row, row, row, _const_spec((1, d)), wspec, wspec, wspec, wspec, wspec],
        out_specs=row,
        out_shape=jax.ShapeDtypeStruct((n, d), F32),
        compiler_params=_params("parallel"),
        name="merge",
    )(x2d, uc, at, gain.reshape(1, d), w_ga, w_gb, w_cp, w_ap, w_o)


def _layer(x, ffn1_norm, ffn1_w_gate, ffn1_w_up, ffn1_w_down, mix_norm, w_in,
           conv_dw, conv_dw_bias, conv_ln_gain, conv_ln_bias, w_conv_proj,
           q_norm, k_norm, w_attn_proj, w_out,
           ffn2_norm, ffn2_w_gate, ffn2_w_up, ffn2_w_down):
    b, s, d = x.shape
    n = b * s
    assert d == N_HEADS * HEAD_DIM and s % TOKEN_TILE == 0
    assert s % (ATTN_QBLOCKS * MOBA_BLOCK) == 0 and s // MOBA_BLOCK <= LANES
    x1 = _ffn(x.reshape(n, d), ffn1_norm, ffn1_w_gate, ffn1_w_up, ffn1_w_down)
    w_in_b = w_in.astype(BF16)
    w5 = w_in_b[:, :5 * d].reshape(d, 5, d).transpose(1, 0, 2)
    u, q, k, v = _mix_in(x1, mix_norm, w5, q_norm, k_norm)
    uc = _conv(u.reshape(b, s, d), conv_dw, conv_dw_bias, conv_ln_gain, conv_ln_bias)
    at = _attn(q.reshape(b, s, d), k.reshape(b, s, d), v.reshape(b, s, d))
    x2 = _merge(x1, uc.reshape(n, d), at.reshape(n, d), mix_norm,
                w_in_b[:, 5 * d:6 * d], w_in_b[:, 6 * d:7 * d],
                w_conv_proj.astype(BF16), w_attn_proj.astype(BF16), w_out.astype(BF16))
    x3 = _ffn(x2, ffn2_norm, ffn2_w_gate, ffn2_w_up, ffn2_w_down)
    return x3.reshape(b, s, d)


def kernel(x, ffn1_norm, ffn1_w_gate, ffn1_w_up, ffn1_w_down, mix_norm, w_in, conv_dw, conv_dw_bias, conv_ln_gain, conv_ln_bias, w_conv_proj, q_norm, k_norm, w_attn_proj, w_out, ffn2_norm, ffn2_w_gate, ffn2_w_up, ffn2_w_down):
    for i in range(ffn1_norm.shape[0]):
        x = _layer(
            x, ffn1_norm[i], ffn1_w_gate[i], ffn1_w_up[i], ffn1_w_down[i], mix_norm[i], w_in[i],
            conv_dw[i], conv_dw_bias[i], conv_ln_gain[i], conv_ln_bias[i], w_conv_proj[i],
            q_norm[i], k_norm[i], w_attn_proj[i], w_out[i],
            ffn2_norm[i], ffn2_w_gate[i], ffn2_w_up[i], ffn2_w_down[i])
    return x
```

```python
import math

import jax
import jax.numpy as jnp
from jax import lax
from jax.experimental import pallas as pl
from jax.experimental.pallas import tpu as pltpu

F32 = jnp.float32
BF16 = jnp.bfloat16

N_HEADS = 16
HEAD_DIM = 64
CONV_WIDTH = 31
MOBA_BLOCK = 256
MOBA_TOPK = 3
EPS = 1e-6
NEG_INF = -1e30
Q_SCALE = HEAD_DIM ** -0.5 * math.log2(math.e)

LANES = 128
SUBLANES = 8
HALO = 32
TOKEN_TILE = 512
FF_CHUNK = 256
CONV_ROWS = 64
ATTN_QBLOCKS = 2
GATE_UNROLL = 4
ATTN_UNROLL = 4
VMEM_LIMIT = 56 * 1024 * 1024


def _params(*sem):
    return pltpu.CompilerParams(dimension_semantics=sem, vmem_limit_bytes=VMEM_LIMIT)


def _const_spec(shape):
    nd = len(shape)
    return pl.BlockSpec(shape, lambda *_: (0,) * nd, pipeline_mode=pl.Buffered(1))


def _rms_rows(x, gain):
    ms = jnp.mean(x * x, axis=-1, keepdims=True)
    return x * lax.rsqrt(ms + EPS) * gain


def _dot(a, b):
    return jnp.dot(a, b, preferred_element_type=F32)


def _ffn_kernel(x_ref, g_ref, wg_ref, wu_ref, wd_ref, o_ref):
    x = x_ref[...]
    h = _rms_rows(x, g_ref[...]).astype(BF16)

    def body(c, acc):
        g = _dot(h, wg_ref[c])
        u = _dot(h, wu_ref[c])
        a = (g * jax.nn.sigmoid(g) * u).astype(BF16)
        return acc + _dot(a, wd_ref[c])

    acc = lax.fori_loop(0, wg_ref.shape[0], body, jnp.zeros(x.shape, F32), unroll=True)
    o_ref[...] = x + 0.5 * acc


def _ffn(x2d, gain, w_gate, w_up, w_down):
    n, d = x2d.shape
    f = w_gate.shape[1]
    nc = f // FF_CHUNK
    wg = w_gate.astype(BF16).reshape(d, nc, FF_CHUNK).transpose(1, 0, 2)
    wu = w_up.astype(BF16).reshape(d, nc, FF_CHUNK).transpose(1, 0, 2)
    wd = w_down.astype(BF16).reshape(nc, FF_CHUNK, d)
    row = pl.BlockSpec((TOKEN_TILE, d), lambda i: (i, 0))
    return pl.pallas_call(
        _ffn_kernel,
        grid=(n // TOKEN_TILE,),
        in_specs=[row, _const_spec((1, d)), _const_spec(wg.shape), _const_spec(wu.shape),
                  _const_spec(wd.shape)],
        out_specs=row,
        out_shape=jax.ShapeDtypeStruct((n, d), F32),
        compiler_params=_params("parallel"),
        name="ffn",
    )(x2d, gain.reshape(1, d), wg, wu, wd)


def _mix_in_kernel(x_ref, g_ref, w_ref, hs_ref, qg_ref, kg_ref, u_ref, q_ref, k_ref, v_ref):
    h = _rms_rows(x_ref[...], g_ref[...]).astype(BF16)
    a = _dot(h, w_ref[0])
    b = _dot(h, w_ref[1])
    u_ref[...] = (a * jax.nn.sigmoid(b)).astype(BF16)

    def head_norm(t, gain):
        ms = _dot((t * t).astype(BF16), hs_ref[...]) * (1.0 / HEAD_DIM)
        return t * lax.rsqrt(ms + EPS) * gain

    q = head_norm(_dot(h, w_ref[2]), qg_ref[...])
    q_ref[...] = (q * Q_SCALE).astype(BF16)
    k_ref[...] = head_norm(_dot(h, w_ref[3]), kg_ref[...]).astype(BF16)
    v_ref[...] = _dot(h, w_ref[4]).astype(BF16)


def _mix_in(x2d, gain, w5, q_gain, k_gain):
    n, d = x2d.shape
    lane_head = jnp.arange(d) // HEAD_DIM
    head_sum = (lane_head[:, None] == lane_head[None, :]).astype(BF16)
    row = pl.BlockSpec((TOKEN_TILE, d), lambda i: (i, 0))
    out = jax.ShapeDtypeStruct((n, d), BF16)
    return pl.pallas_call(
        _mix_in_kernel,
        grid=(n // TOKEN_TILE,),
        in_specs=[row, _const_spec((1, d)), _const_spec(w5.shape), _const_spec((d, d)),
                  _const_spec((1, d)), _const_spec((1, d))],
        out_specs=[row, row, row, row],
        out_shape=[out, out, out, out],
        compiler_params=_params("parallel"),
        name="mix_in",
    )(x2d, gain.reshape(1, d), w5, head_sum,
      jnp.tile(q_gain, N_HEADS).reshape(1, d), jnp.tile(k_gain, N_HEADS).reshape(1, d))


def _conv_kernel(u_ref, halo_ref, w_ref, b_ref, lg_ref, lb_ref, o_ref, sh_ref, conv_ref):
    rows = u_ref.shape[1]
    strips = u_ref.shape[2] // LANES
    total = HALO + rows
    first = pl.program_id(1) == 0
    halo = halo_ref[0].astype(F32)
    halo = jnp.where(first, jnp.zeros_like(halo), halo)
    u = u_ref[0].astype(F32)
    for t in range(strips):
        lanes = slice(t * LANES, (t + 1) * LANES)
        sh_ref[0, t, 0:HALO, :] = halo[:, lanes]
        sh_ref[0, t, HALO:total, :] = u[:, lanes]
        for r in range(1, SUBLANES):
            sh_ref[r, t, 0:total - r, :] = sh_ref[0, t, r:total, :]
    lead = HALO - (CONV_WIDTH - 1)
    groups = CONV_ROWS // SUBLANES

    def body(idx, carry):
        t = idx % strips
        base = pl.multiple_of((idx // strips) * CONV_ROWS, CONV_ROWS)
        accs = [b_ref[t]] * groups
        for shift in range(SUBLANES):
            taps = range((shift - lead) % SUBLANES, CONV_WIDTH, SUBLANES)
            ws = {j: w_ref[j, t] for j in taps}
            offs = sorted({lead + j - shift + g * SUBLANES for j in taps for g in range(groups)})
            for off in offs:
                rows8 = sh_ref[shift, t, pl.ds(base + off, SUBLANES), :]
                for j in taps:
                    g, rem = divmod(off - (lead + j - shift), SUBLANES)
                    if rem == 0 and 0 <= g < groups:
                        accs[g] = accs[g] + rows8 * ws[j]
        conv_ref[t, pl.ds(base, CONV_ROWS), :] = jnp.concatenate(accs, axis=0)
        return carry

    lax.fori_loop(0, (rows // CONV_ROWS) * strips, body, 0)
    acc = jnp.concatenate([conv_ref[t] for t in range(strips)], axis=1)
    mu = jnp.mean(acc, axis=-1, keepdims=True)
    cen = acc - mu
    var = jnp.mean(cen * cen, axis=-1, keepdims=True)
    y = cen * lax.rsqrt(var + EPS) * lg_ref[...] + lb_ref[...]
    o_ref[0] = (y * jax.nn.sigmoid(y)).astype(BF16)


def _conv(u3d, w, bias, ln_gain, ln_bias):
    b, s, c = u3d.shape
    strips = c // LANES
    per = TOKEN_TILE // HALO
    tile = pl.BlockSpec((1, TOKEN_TILE, c), lambda bi, i: (bi, i, 0))
    halo = pl.BlockSpec((1, HALO, c), lambda bi, i: (bi, jnp.maximum(i * per - 1, 0), 0))
    rep = lambda p: jnp.broadcast_to(p.reshape(-1, strips, 1, LANES),
                                     (p.size // c, strips, SUBLANES, LANES))
    vec = _const_spec((1, c))
    return pl.pallas_call(
        _conv_kernel,
        grid=(b, s // TOKEN_TILE),
        in_specs=[tile, halo, _const_spec((CONV_WIDTH, strips, SUBLANES, LANES)),
                  _const_spec((strips, SUBLANES, LANES)), vec, vec],
        out_specs=tile,
        out_shape=jax.ShapeDtypeStruct((b, s, c), BF16),
        scratch_shapes=[pltpu.VMEM((SUBLANES, strips, HALO + TOKEN_TILE, LANES), F32),
                        pltpu.VMEM((strips, TOKEN_TILE, LANES), F32)],
        compiler_params=_params("parallel", "parallel"),
        name="conv",
    )(u3d, u3d, rep(w), rep(bias)[0], ln_gain.reshape(1, c), ln_bias.reshape(1, c))


def _attn_kernel(q_ref, k_ref, v_ref, o_ref, kaug_ref, vaug_ref, kmean_ref, qaug_ref,
                 sa_ref, sb_ref, ra_ref, rb_ref, m_ref, acc_ref):
    a = pl.program_id(2)
    s_len = k_ref.shape[1]
    nb = s_len // MOBA_BLOCK
    blk = MOBA_BLOCK
    span = ATTN_QBLOCKS * blk
    rows = 2 * span

    nt = (((1,), (1,)), ((), ()))

    def build_qaug(step):
        qp = q_ref[0, pl.ds(pl.multiple_of(step * span, span), span), :]
        q2 = jnp.concatenate([qp, qp], axis=0)
        row_head = lax.broadcasted_iota(jnp.int32, (rows, LANES), 0) // span
        lane_head = lax.broadcasted_iota(jnp.int32, (rows, LANES), 1) // HEAD_DIM
        qe = jnp.where(row_head == lane_head, q2, jnp.zeros_like(q2))
        cand = lax.broadcasted_iota(jnp.int32, (nb, rows), 0)
        q_blk = step * ATTN_QBLOCKS + (lax.broadcasted_iota(jnp.int32, (nb, rows), 1) % span) // blk
        gate = lax.dot_general(kmean_ref[...], qe, nt, preferred_element_type=F32)
        gate = jnp.where(cand < q_blk, gate, -jnp.inf)
        sel = cand == q_blk
        for _ in range(MOBA_TOPK):
            mx = jnp.max(gate, axis=0, keepdims=True)
            first = jnp.min(jnp.where(gate == mx, cand, nb), axis=0, keepdims=True)
            pick = (cand == first) & (mx > -jnp.inf)
            sel = sel | pick
            gate = jnp.where(pick, -jnp.inf, gate)
        bias_t = jnp.where(sel, 0.0, NEG_INF).astype(F32)
        bias_t = jnp.concatenate([bias_t, jnp.zeros((LANES - nb, rows), F32)], axis=0)
        return jnp.concatenate([qe, bias_t.T.astype(BF16)], axis=1)

    @pl.when(a == 0)
    def _():
        k = k_ref[0]
        kaug_ref[:, 0:LANES] = k
        row_blk = lax.broadcasted_iota(jnp.int32, (s_len, LANES), 0) // blk
        lane = lax.broadcasted_iota(jnp.int32, (s_len, LANES), 1)
        kaug_ref[:, LANES:2 * LANES] = (row_blk == lane).astype(BF16)
        vaug_ref[:, 0:LANES] = v_ref[0]
        vaug_ref[:, LANES:2 * LANES] = jnp.ones((s_len, LANES), BF16)
        kmean_ref[...] = jnp.mean(k.astype(F32).reshape(nb, blk, LANES), axis=1).astype(BF16)

        def gate_step(step, carry):
            qaug_ref[step] = build_qaug(step)
            return carry

        lax.fori_loop(0, s_len // span, gate_step, 0, unroll=GATE_UNROLL)

    def issue_scores(n, s_ref, rmax_ref):
        start = pl.multiple_of(n * span, span)
        s = lax.dot_general(qaug_ref[a], kaug_ref[pl.ds(start, span), :], nt,
                            preferred_element_type=F32)
        s_ref[...] = s
        rmax_ref[...] = jnp.broadcast_to(jnp.max(s, axis=1, keepdims=True), rmax_ref.shape)

    def absorb(n, s_ref, rmax_ref, own):
        s = s_ref[...]
        if own:
            q_pos = lax.broadcasted_iota(jnp.int32, (rows, span), 0) % span
            k_pos = lax.broadcasted_iota(jnp.int32, (rows, span), 1)
            s = jnp.where(k_pos <= q_pos, s, NEG_INF)
            rmax = jnp.broadcast_to(jnp.max(s, axis=1, keepdims=True), m_ref.shape)
        else:
            rmax = rmax_ref[...]
        m_old = m_ref[...]
        m_new = jnp.maximum(m_old, rmax)
        alpha = jnp.exp2(m_old - m_new)
        p = jnp.exp2(s - jnp.concatenate([m_new] * (span // LANES), axis=1))
        start = pl.multiple_of(n * span, span)
        acc_ref[...] = (jnp.concatenate([alpha, alpha], axis=1) * acc_ref[...]
                        + _dot(p.astype(BF16), vaug_ref[pl.ds(start, span), :]))
        m_ref[...] = m_new

    m_ref[...] = jnp.full(m_ref.shape, 0.1 * NEG_INF, F32)
    acc_ref[...] = jnp.zeros_like(acc_ref)
    bufs = ((sa_ref, ra_ref), (sb_ref, rb_ref))
    issue_scores(0, *bufs[0])

    def run_past(first, count):
        for i in range(count):
            issue_scores(first + i + 1, *bufs[(i + 1) % 2])
            absorb(first + i, *bufs[i % 2], False)

    def unrolled(t, carry):
        run_past(ATTN_UNROLL * t, ATTN_UNROLL)
        return carry

    lax.fori_loop(0, a // ATTN_UNROLL, unrolled, 0)
    done = (a // ATTN_UNROLL) * ATTN_UNROLL
    for left in range(ATTN_UNROLL):
        @pl.when(a - done == left)
        def _(left=left):
            run_past(done, left)
            absorb(a, *bufs[left % 2], True)

    out = acc_ref[:, 0:LANES] / acc_ref[:, LANES:2 * LANES]
    lane_q = lax.broadcasted_iota(jnp.int32, (span, LANES), 1)
    o_ref[0] = jnp.where(lane_q < HEAD_DIM, out[0:span], out[span:rows]).astype(BF16)


def _attn(q, k, v):
    b, s, d = q.shape
    pairs = d // LANES
    nb = s // MOBA_BLOCK
    span = ATTN_QBLOCKS * MOBA_BLOCK
    rows = 2 * span
    steps = s // span
    ospec = pl.BlockSpec((1, span, LANES), lambda bi, hp, i: (bi, i, hp))
    seq = pl.BlockSpec((1, s, LANES), lambda bi, hp, i: (bi, 0, hp))
    return pl.pallas_call(
        _attn_kernel,
        grid=(b, pairs, steps),
        in_specs=[seq, seq, seq],
        out_specs=ospec,
        out_shape=jax.ShapeDtypeStruct((b, s, d), BF16),
        scratch_shapes=[pltpu.VMEM((s, 2 * LANES), BF16), pltpu.VMEM((s, 2 * LANES), BF16),
                        pltpu.VMEM((nb, LANES), BF16),
                        pltpu.VMEM((steps, rows, 2 * LANES), BF16),
                        pltpu.VMEM((rows, span), F32), pltpu.VMEM((rows, span), F32),
                        pltpu.VMEM((rows, LANES), F32), pltpu.VMEM((rows, LANES), F32),
                        pltpu.VMEM((rows, LANES), F32), pltpu.VMEM((rows, 2 * LANES), F32)],
        compiler_params=_params("parallel", "parallel", "arbitrary"),
        name="moba_attn",
    )(q, k, v)


def _merge_kernel(x_ref, uc_ref, at_ref, g_ref, wga_ref, wgb_ref, wcp_ref, wap_ref, wo_ref, o_ref):
    x = x_ref[...]
    h = _rms_rows(x, g_ref[...]).astype(BF16)
    ya = jax.nn.sigmoid(_dot(h, wga_ref[...])) * _dot(uc_ref[...], wcp_ref[...])
    yb = jax.nn.sigmoid(_dot(h, wgb_ref[...])) * _dot(at_ref[...], wap_ref[...])
    o_ref[...] = x + _dot((ya + yb).astype(BF16), wo_ref[...])


def _merge(x2d, uc, at, gain, w_ga, w_gb, w_cp, w_ap, w_o):
    n, d = x2d.shape
    row = pl.BlockSpec((TOKEN_TILE, d), lambda i: (i, 0))
    wspec = _const_spec((d, d))
    return pl.pallas_call(
        _merge_kernel,
        grid=(n // TOKEN_TILE,),
        in_specs=[row, row, row, _const_spec((1, d)), wspec, wspec, wspec, wspec, wspec],
        out_specs=row,
        out_shape=jax.ShapeDtypeStruct((n, d), F32),
        compiler_params=_params("parallel"),
        name="merge",
    )(x2d, uc, at, gain.reshape(1, d), w_ga, w_gb, w_cp, w_ap, w_o)


def _layer(x, ffn1_norm, ffn1_w_gate, ffn1_w_up, ffn1_w_down, mix_norm, w_in,
           conv_dw, conv_dw_bias, conv_ln_gain, conv_ln_bias, w_conv_proj,
           q_norm, k_norm, w_attn_proj, w_out,
           ffn2_norm, ffn2_w_gate, ffn2_w_up, ffn2_w_down):
    b, s, d = x.shape
    n = b * s
    assert d == N_HEADS * HEAD_DIM and s % TOKEN_TILE == 0
    assert s % (ATTN_QBLOCKS * MOBA_BLOCK) == 0 and s // MOBA_BLOCK <= LANES
    x1 = _ffn(x.reshape(n, d), ffn1_norm, ffn1_w_gate, ffn1_w_up, ffn1_w_down)
    w_in_b = w_in.astype(BF16)
    w5 = w_in_b[:, :5 * d].reshape(d, 5, d).transpose(1, 0, 2)
    u, q, k, v = _mix_in(x1, mix_norm, w5, q_norm, k_norm)
    uc = _conv(u.reshape(b, s, d), conv_dw, conv_dw_bias, conv_ln_gain, conv_ln_bias)
    at = _attn(q.reshape(b, s, d), k.reshape(b, s, d), v.reshape(b, s, d))
    x2 = _merge(x1, uc.reshape(n, d), at.reshape(n, d), mix_norm,
                w_in_b[:, 5 * d:6 * d], w_in_b[:, 6 * d:7 * d],
                w_conv_proj.astype(BF16), w_attn_proj.astype(BF16), w_out.astype(BF16))
    x3 = _ffn(x2, ffn2_norm, ffn2_w_gate, ffn2_w_up, ffn2_w_down)
    return x3.reshape(b, s, d)


def kernel(x, ffn1_norm, ffn1_w_gate, ffn1_w_up, ffn1_w_down, mix_norm, w_in, conv_dw, conv_dw_bias, conv_ln_gain, conv_ln_bias, w_conv_proj, q_norm, k_norm, w_attn_proj, w_out, ffn2_norm, ffn2_w_gate, ffn2_w_up, ffn2_w_down):
    for i in range(ffn1_norm.shape[0]):
        x = _layer(
            x, ffn1_norm[i], ffn1_w_gate[i], ffn1_w_up[i], ffn1_w_down[i], mix_norm[i], w_in[i],
            conv_dw[i], conv_dw_bias[i], conv_ln_gain[i], conv_ln_bias[i], w_conv_proj[i],
            q_norm[i], k_norm[i], w_attn_proj[i], w_out[i],
            ffn2_norm[i], ffn2_w_gate[i], ffn2_w_up[i], ffn2_w_down[i])
    return x
```

```python
import math

import jax
import jax.numpy as jnp
from jax import lax
from jax.experimental import pallas as pl
from jax.experimental.pallas import tpu as pltpu

F32 = jnp.float32
BF16 = jnp.bfloat16

N_HEADS = 16
HEAD_DIM = 64
CONV_WIDTH = 31
MOBA_BLOCK = 256
MOBA_TOPK = 3
EPS = 1e-6
NEG_INF = -1e30
Q_SCALE = HEAD_DIM ** -0.5 * math.log2(math.e)

LANES = 128
SUBLANES = 8
HALO = 32
TOKEN_TILE = 512
FF_CHUNK = 256
CONV_ROWS = 128
ATTN_QBLOCKS = 2
GATE_UNROLL = 4
ATTN_UNROLL = 4
VMEM_LIMIT = 56 * 1024 * 1024


def _params(*sem):
    return pltpu.CompilerParams(dimension_semantics=sem, vmem_limit_bytes=VMEM_LIMIT)


def _const_spec(shape):
    nd = len(shape)
    return pl.BlockSpec(shape, lambda *_: (0,) * nd, pipeline_mode=pl.Buffered(1))


def _rms_rows(x, gain):
    ms = jnp.mean(x * x, axis=-1, keepdims=True)
    return x * lax.rsqrt(ms + EPS) * gain


def _dot(a, b):
    return jnp.dot(a, b, preferred_element_type=F32)


def _ffn_kernel(x_ref, g_ref, wg_ref, wu_ref, wd_ref, o_ref):
    x = x_ref[...]
    h = _rms_rows(x, g_ref[...]).astype(BF16)

    acc = jnp.zeros(x.shape, F32)
    for c in range(0, wg_ref.shape[1], FF_CHUNK):
        cols = slice(c, c + FF_CHUNK)
        g = _dot(h, wg_ref[:, cols])
        u = _dot(h, wu_ref[:, cols])
        a = (g * jax.nn.sigmoid(g) * u).astype(BF16)
        acc = acc + _dot(a, wd_ref[cols, :])
    o_ref[...] = x + 0.5 * acc


def _ffn(x2d, gain, w_gate, w_up, w_down):
    n, d = x2d.shape
    assert w_gate.shape[1] % FF_CHUNK == 0
    wg, wu, wd = w_gate.astype(BF16), w_up.astype(BF16), w_down.astype(BF16)
    row = pl.BlockSpec((TOKEN_TILE, d), lambda i: (i, 0))
    return pl.pallas_call(
        _ffn_kernel,
        grid=(n // TOKEN_TILE,),
        in_specs=[row, _const_spec((1, d)), _const_spec(wg.shape), _const_spec(wu.shape),
                  _const_spec(wd.shape)],
        out_specs=row,
        out_shape=jax.ShapeDtypeStruct((n, d), F32),
        compiler_params=_params("parallel"),
        name="ffn",
    )(x2d, gain.reshape(1, d), wg, wu, wd)


def _mix_in_kernel(x_ref, g_ref, w_ref, hs_ref, hx_ref, qg_ref, kg_ref, u_ref, q_ref, k_ref, v_ref):
    h = _rms_rows(x_ref[...], g_ref[...]).astype(BF16)
    d = h.shape[1]
    w = lambda j: w_ref[:, j * d:(j + 1) * d]
    def head_norm(t, gain):
        ss = _dot((t * t).astype(BF16), hs_ref[...])
        hi = ss.astype(BF16)
        lo = (ss - hi.astype(F32)).astype(BF16)
        ms = (_dot(hi, hx_ref[...]) + _dot(lo, hx_ref[...])) * (1.0 / HEAD_DIM)
        return t * lax.rsqrt(ms + EPS) * gain

    q = head_norm(_dot(h, w(2)), qg_ref[...])
    k = head_norm(_dot(h, w(3)), kg_ref[...])
    q_ref[...] = (q * Q_SCALE).astype(BF16)
    k_ref[...] = k.astype(BF16)
    a = _dot(h, w(0))
    b = _dot(h, w(1))
    u_ref[...] = (a * jax.nn.sigmoid(b)).astype(BF16)
    v_ref[...] = _dot(h, w(4)).astype(BF16)


def _mix_in(x2d, gain, w_in_b, q_gain, k_gain):
    n, d = x2d.shape
    lane_head = jnp.arange(d) // HEAD_DIM
    head_sum = (lane_head[:, None] == jnp.arange(LANES)[None, :]).astype(BF16)
    head_spread = head_sum.T
    row = pl.BlockSpec((TOKEN_TILE, d), lambda i: (i, 0))
    out = jax.ShapeDtypeStruct((n, d), BF16)
    return pl.pallas_call(
        _mix_in_kernel,
        grid=(n // TOKEN_TILE,),
        in_specs=[row, _const_spec((1, d)), _const_spec((d, 5 * d)), _const_spec((d, LANES)),
                  _const_spec((LANES, d)), _const_spec((1, d)), _const_spec((1, d))],
        out_specs=[row, row, row, row],
        out_shape=[out, out, out, out],
        compiler_params=_params("parallel"),
        name="mix_in",
    )(x2d, gain.reshape(1, d), w_in_b, head_sum, head_spread,
      jnp.tile(q_gain, N_HEADS).reshape(1, d), jnp.tile(k_gain, N_HEADS).reshape(1, d))


def _conv_kernel(u_ref, halo_ref, w_ref, b_ref, lg_ref, lb_ref, o_ref, sh_ref, conv_ref):
    rows = u_ref.shape[1]
    strips = u_ref.shape[2] // LANES
    total = HALO + rows
    first = pl.program_id(1) == 0
    halo = halo_ref[0].astype(F32)
    halo = jnp.where(first, jnp.zeros_like(halo), halo)
    u = u_ref[0].astype(F32)
    for t in range(strips):
        lanes = slice(t * LANES, (t + 1) * LANES)
        sh_ref[0, t, 0:HALO, :] = halo[:, lanes]
        sh_ref[0, t, HALO:total, :] = u[:, lanes]
        for r in range(1, SUBLANES):
            sh_ref[r, t, 0:total - r, :] = sh_ref[0, t, r:total, :]
    lead = HALO - (CONV_WIDTH - 1)
    groups = CONV_ROWS // SUBLANES

    def body(idx, carry):
        t = idx % strips
        base = pl.multiple_of((idx // strips) * CONV_ROWS, CONV_ROWS)
        accs = [b_ref[t]] * groups
        for shift in range(SUBLANES):
            taps = range((shift - lead) % SUBLANES, CONV_WIDTH, SUBLANES)
            ws = {j: w_ref[j, t] for j in taps}
            offs = sorted({lead + j - shift + g * SUBLANES for j in taps for g in range(groups)})
            for off in offs:
                rows8 = sh_ref[shift, t, pl.ds(base + off, SUBLANES), :]
                for j in taps:
                    g, rem = divmod(off - (lead + j - shift), SUBLANES)
                    if rem == 0 and 0 <= g < groups:
                        accs[g] = accs[g] + rows8 * ws[j]
        conv_ref[t, pl.ds(base, CONV_ROWS), :] = jnp.concatenate(accs, axis=0)
        return carry

    lax.fori_loop(0, (rows // CONV_ROWS) * strips, body, 0)
    acc = jnp.concatenate([conv_ref[t] for t in range(strips)], axis=1)
    mu = jnp.mean(acc, axis=-1, keepdims=True)
    cen = acc - mu
    var = jnp.mean(cen * cen, axis=-1, keepdims=True)
    y = cen * lax.rsqrt(var + EPS) * lg_ref[...] + lb_ref[...]
    o_ref[0] = (y * jax.nn.sigmoid(y)).astype(BF16)


def _conv(u3d, w, bias, ln_gain, ln_bias):
    b, s, c = u3d.shape
    strips = c // LANES
    per = TOKEN_TILE // HALO
    tile = pl.BlockSpec((1, TOKEN_TILE, c), lambda bi, i: (bi, i, 0))
    halo = pl.BlockSpec((1, HALO, c), lambda bi, i: (bi, jnp.maximum(i * per - 1, 0), 0))
    rep = lambda p: jnp.broadcast_to(p.reshape(-1, strips, 1, LANES),
                                     (p.size // c, strips, SUBLANES, LANES))
    vec = _const_spec((1, c))
    return pl.pallas_call(
        _conv_kernel,
        grid=(b, s // TOKEN_TILE),
        in_specs=[tile, halo, _const_spec((CONV_WIDTH, strips, SUBLANES, LANES)),
                  _const_spec((strips, SUBLANES, LANES)), vec, vec],
        out_specs=tile,
        out_shape=jax.ShapeDtypeStruct((b, s, c), BF16),
        scratch_shapes=[pltpu.VMEM((SUBLANES, strips, HALO + TOKEN_TILE, LANES), F32),
                        pltpu.VMEM((strips, TOKEN_TILE, LANES), F32)],
        compiler_params=_params("parallel", "parallel"),
        name="conv",
    )(u3d, u3d, rep(w), rep(bias)[0], ln_gain.reshape(1, c), ln_bias.reshape(1, c))


def _attn_kernel(q_ref, k_ref, v_ref, o_ref, kaug_ref, vaug_ref, kmean_ref, qaug_ref,
                 sa_ref, sb_ref, ra_ref, rb_ref, m_ref, acc_ref):
    a = pl.program_id(2)
    s_len = k_ref.shape[1]
    nb = s_len // MOBA_BLOCK
    blk = MOBA_BLOCK
    span = ATTN_QBLOCKS * blk
    rows = 2 * span

    nt = (((1,), (1,)), ((), ()))

    def build_qaug(step):
        qp = q_ref[0, pl.ds(pl.multiple_of(step * span, span), span), :]
        q2 = jnp.concatenate([qp, qp], axis=0)
        row_head = lax.broadcasted_iota(jnp.int32, (rows, LANES), 0) // span
        lane_head = lax.broadcasted_iota(jnp.int32, (rows, LANES), 1) // HEAD_DIM
        qe = jnp.where(row_head == lane_head, q2, jnp.zeros_like(q2))
        cand = lax.broadcasted_iota(jnp.int32, (nb, rows), 0)
        q_blk = step * ATTN_QBLOCKS + (lax.broadcasted_iota(jnp.int32, (nb, rows), 1) % span) // blk
        gate = lax.dot_general(kmean_ref[...], qe, nt, preferred_element_type=F32)
        gate = jnp.where(cand < q_blk, gate, -jnp.inf)
        sel = cand == q_blk
        for _ in range(MOBA_TOPK):
            mx = jnp.max(gate, axis=0, keepdims=True)
            first = jnp.min(jnp.where(gate == mx, cand, nb), axis=0, keepdims=True)
            pick = (cand == first) & (mx > -jnp.inf)
            sel = sel | pick
            gate = jnp.where(pick, -jnp.inf, gate)
        bias_t = jnp.where(sel, 0.0, NEG_INF).astype(F32)
        bias_t = jnp.concatenate([bias_t, jnp.zeros((LANES - nb, rows), F32)], axis=0)
        return jnp.concatenate([qe, bias_t.T.astype(BF16)], axis=1)

    @pl.when(a == 0)
    def _():
        k = k_ref[0]
        kaug_ref[:, 0:LANES] = k
        row_blk = lax.broadcasted_iota(jnp.int32, (s_len, LANES), 0) // blk
        lane = lax.broadcasted_iota(jnp.int32, (s_len, LANES), 1)
        kaug_ref[:, LANES:2 * LANES] = (row_blk == lane).astype(BF16)
        vaug_ref[:, 0:LANES] = v_ref[0]
        vaug_ref[:, LANES:2 * LANES] = jnp.ones((s_len, LANES), BF16)
        kmean_ref[...] = jnp.mean(k.astype(F32).reshape(nb, blk, LANES), axis=1).astype(BF16)

        def gate_step(step, carry):
            qaug_ref[step] = build_qaug(step)
            return carry

        lax.fori_loop(0, s_len // span, gate_step, 0, unroll=GATE_UNROLL)

    def issue_scores(n, s_ref, rmax_ref):
        start = pl.multiple_of(n * span, span)
        s = lax.dot_general(qaug_ref[a], kaug_ref[pl.ds(start, span), :], nt,
                            preferred_element_type=F32)
        s_ref[...] = s
        rmax_ref[...] = jnp.broadcast_to(jnp.max(s, axis=1, keepdims=True), rmax_ref.shape)

    def absorb(n, s_ref, rmax_ref, own):
        s = s_ref[...]
        if own:
            q_pos = lax.broadcasted_iota(jnp.int32, (rows, span), 0) % span
            k_pos = lax.broadcasted_iota(jnp.int32, (rows, span), 1)
            s = jnp.where(k_pos <= q_pos, s, NEG_INF)
            rmax = jnp.broadcast_to(jnp.max(s, axis=1, keepdims=True), m_ref.shape)
        else:
            rmax = rmax_ref[...]
        m_old = m_ref[...]
        m_new = jnp.maximum(m_old, rmax)
        alpha = jnp.exp2(m_old - m_new)
        p = jnp.exp2(s - jnp.concatenate([m_new] * (span // LANES), axis=1))
        start = pl.multiple_of(n * span, span)
        acc_ref[...] = (jnp.concatenate([alpha, alpha], axis=1) * acc_ref[...]
                        + _dot(p.astype(BF16), vaug_ref[pl.ds(start, span), :]))
        m_ref[...] = m_new

    m_ref[...] = jnp.full(m_ref.shape, 0.1 * NEG_INF, F32)
    acc_ref[...] = jnp.zeros_like(acc_ref)
    bufs = ((sa_ref, ra_ref), (sb_ref, rb_ref))
    issue_scores(0, *bufs[0])

    def run_past(first, count):
        for i in range(count):
            issue_scores(first + i + 1, *bufs[(i + 1) % 2])
            absorb(first + i, *bufs[i % 2], False)

    def unrolled(t, carry):
        run_past(ATTN_UNROLL * t, ATTN_UNROLL)
        return carry

    lax.fori_loop(0, a // ATTN_UNROLL, unrolled, 0)
    done = (a // ATTN_UNROLL) * ATTN_UNROLL
    for left in range(ATTN_UNROLL):
        @pl.when(a - done == left)
        def _(left=left):
            run_past(done, left)
            absorb(a, *bufs[left % 2], True)

    out = acc_ref[:, 0:LANES] / acc_ref[:, LANES:2 * LANES]
    lane_q = lax.broadcasted_iota(jnp.int32, (span, LANES), 1)
    o_ref[0] = jnp.where(lane_q < HEAD_DIM, out[0:span], out[span:rows]).astype(BF16)


def _attn(q, k, v):
    b, s, d = q.shape
    pairs = d // LANES
    nb = s // MOBA_BLOCK
    span = ATTN_QBLOCKS * MOBA_BLOCK
    rows = 2 * span
    steps = s // span
    ospec = pl.BlockSpec((1, span, LANES), lambda bi, hp, i: (bi, i, hp))
    seq = pl.BlockSpec((1, s, LANES), lambda bi, hp, i: (bi, 0, hp))
    return pl.pallas_call(
        _attn_kernel,
        grid=(b, pairs, steps),
        in_specs=[seq, seq, seq],
        out_specs=ospec,
        out_shape=jax.ShapeDtypeStruct((b, s, d), BF16),
        scratch_shapes=[pltpu.VMEM((s, 2 * LANES), BF16), pltpu.VMEM((s, 2 * LANES), BF16),
                        pltpu.VMEM((nb, LANES), BF16),
                        pltpu.VMEM((steps, rows, 2 * LANES), BF16),
                        pltpu.VMEM((rows, span), F32), pltpu.VMEM((rows, span), F32),
                        pltpu.VMEM((rows, LANES), F32), pltpu.VMEM((rows, LANES), F32),
                        pltpu.VMEM((rows, LANES), F32), pltpu.VMEM((rows, 2 * LANES), F32)],
        compiler_params=_params("parallel", "parallel", "arbitrary"),
        name="moba_attn",
    )(q, k, v)


def _merge_kernel(x_ref, uc_ref, at_ref, g_ref, wga_ref, wgb_ref, wcp_ref, wap_ref, wo_ref, o_ref):
    x = x_ref[...]
    h = _rms_rows(x, g_ref[...]).astype(BF16)
    ya = jax.nn.sigmoid(_dot(h, wga_ref[...])) * _dot(uc_ref[...], wcp_ref[...])
    yb = jax.nn.sigmoid(_dot(h, wgb_ref[...])) * _dot(at_ref[...], wap_ref[...])
    o_ref[...] = x + _dot((ya + yb).astype(BF16), wo_ref[...])


def _merge(x2d, uc, at, gain, w_in_b, w_cp, w_ap, w_o):
    n, d = x2d.shape
    row = pl.BlockSpec((TOKEN_TILE, d), lambda i: (i, 0))
    wspec = _const_spec((d, d))
    gate_cols = lambda j: pl.BlockSpec((d, d), lambda i: (0, j), pipeline_mode=pl.Buffered(1))
    return pl.pallas_call(
        _merge_kernel,
        grid=(n // TOKEN_TILE,),
        in_specs=[row, row, row, _const_spec((1, d)), gate_cols(5), gate_cols(6),
                  wspec, wspec, wspec],
        out_specs=row,
        out_shape=jax.ShapeDtypeStruct((n, d), F32),
        compiler_params=_params("parallel"),
        name="merge",
    )(x2d, uc, at, gain.reshape(1, d), w_in_b, w_in_b, w_cp, w_ap, w_o)


def _layer(x, ffn1_norm, ffn1_w_gate, ffn1_w_up, ffn1_w_down, mix_norm, w_in,
           conv_dw, conv_dw_bias, conv_ln_gain, conv_ln_bias, w_conv_proj,
           q_norm, k_norm, w_attn_proj, w_out,
           ffn2_norm, ffn2_w_gate, ffn2_w_up, ffn2_w_down):
    b, s, d = x.shape
    n = b * s
    assert d == N_HEADS * HEAD_DIM and s % TOKEN_TILE == 0
    assert s % (ATTN_QBLOCKS * MOBA_BLOCK) == 0 and s // MOBA_BLOCK <= LANES
    x1 = _ffn(x.reshape(n, d), ffn1_norm, ffn1_w_gate, ffn1_w_up, ffn1_w_down)
    w_in_b = w_in.astype(BF16)
    u, q, k, v = _mix_in(x1, mix_norm, w_in_b, q_norm, k_norm)
    uc = _conv(u.reshape(b, s, d), conv_dw, conv_dw_bias, conv_ln_gain, conv_ln_bias)
    at = _attn(q.reshape(b, s, d), k.reshape(b, s, d), v.reshape(b, s, d))
    x2 = _merge(x1, uc.reshape(n, d), at.reshape(n, d), mix_norm, w_in_b,
                w_conv_proj.astype(BF16), w_attn_proj.astype(BF16), w_out.astype(BF16))
    x3 = _ffn(x2, ffn2_norm, ffn2_w_gate, ffn2_w_up, ffn2_w_down)
    return x3.reshape(b, s, d)


def kernel(x, ffn1_norm, ffn1_w_gate, ffn1_w_up, ffn1_w_down, mix_norm, w_in, conv_dw, conv_dw_bias, conv_ln_gain, conv_ln_bias, w_conv_proj, q_norm, k_norm, w_attn_proj, w_out, ffn2_norm, ffn2_w_gate, ffn2_w_up, ffn2_w_down):
    for i in range(ffn1_norm.shape[0]):
        x = _layer(
            x, ffn1_norm[i], ffn1_w_gate[i], ffn1_w_up[i], ffn1_w_down[i], mix_norm[i], w_in[i],
            conv_dw[i], conv_dw_bias[i], conv_ln_gain[i], conv_ln_bias[i], w_conv_proj[i],
            q_norm[i], k_norm[i], w_attn_proj[i], w_out[i],
            ffn2_norm[i], ffn2_w_gate[i], ffn2_w_up[i], ffn2_w_down[i])
    return x
```

```python
import math

import jax
import jax.numpy as jnp
from jax import lax
from jax.experimental import pallas as pl
from jax.experimental.pallas import tpu as pltpu

F32 = jnp.float32
BF16 = jnp.bfloat16

N_HEADS = 16
HEAD_DIM = 64
CONV_WIDTH = 31
MOBA_BLOCK = 256
MOBA_TOPK = 3
EPS = 1e-6
NEG_INF = -1e30
Q_SCALE = HEAD_DIM ** -0.5 * math.log2(math.e)

LANES = 128
SUBLANES = 8
HALO = 32
TOKEN_TILE = 512
FF_CHUNK = 256
CONV_ROWS = 128
ATTN_QBLOCKS = 2
GATE_UNROLL = 4
ATTN_UNROLL = 4
VMEM_LIMIT = 56 * 1024 * 1024


def _params(*sem):
    return pltpu.CompilerParams(dimension_semantics=sem, vmem_limit_bytes=VMEM_LIMIT)


def _const_spec(shape):
    nd = len(shape)
    return pl.BlockSpec(shape, lambda *_: (0,) * nd, pipeline_mode=pl.Buffered(1))


def _rms_rows(x, gain):
    ms = jnp.mean(x * x, axis=-1, keepdims=True)
    return x * lax.rsqrt(ms + EPS) * gain


def _dot(a, b):
    return jnp.dot(a, b, preferred_element_type=F32)


def _ffn_kernel(x_ref, g_ref, wg_ref, wu_ref, wd_ref, o_ref):
    x = x_ref[...]
    h = _rms_rows(x, g_ref[...]).astype(BF16)

    acc = jnp.zeros(x.shape, F32)
    for c in range(0, wg_ref.shape[1], FF_CHUNK):
        cols = slice(c, c + FF_CHUNK)
        g = _dot(h, wg_ref[:, cols])
        u = _dot(h, wu_ref[:, cols])
        a = (g * jax.nn.sigmoid(g) * u).astype(BF16)
        acc = acc + _dot(a, wd_ref[cols, :])
    o_ref[...] = x + 0.5 * acc


def _ffn(x2d, gain, w_gate, w_up, w_down):
    n, d = x2d.shape
    assert w_gate.shape[1] % FF_CHUNK == 0
    wg, wu, wd = w_gate.astype(BF16), w_up.astype(BF16), w_down.astype(BF16)
    row = pl.BlockSpec((TOKEN_TILE, d), lambda i: (i, 0))
    return pl.pallas_call(
        _ffn_kernel,
        grid=(n // TOKEN_TILE,),
        in_specs=[row, _const_spec((1, d)), _const_spec(wg.shape), _const_spec(wu.shape),
                  _const_spec(wd.shape)],
        out_specs=row,
        out_shape=jax.ShapeDtypeStruct((n, d), F32),
        compiler_params=_params("parallel"),
        name="ffn",
    )(x2d, gain.reshape(1, d), wg, wu, wd)


def _mix_in_kernel(x_ref, g_ref, w_ref, hs_ref, hx_ref, qg_ref, kg_ref, u_ref, q_ref, k_ref, v_ref):
    h = _rms_rows(x_ref[...], g_ref[...]).astype(BF16)
    d = h.shape[1]
    w = lambda j: w_ref[:, j * d:(j + 1) * d]
    def head_norm(t, gain):
        ss = _dot((t * t).astype(BF16), hs_ref[...])
        hi = ss.astype(BF16)
        lo = (ss - hi.astype(F32)).astype(BF16)
        ms = (_dot(hi, hx_ref[...]) + _dot(lo, hx_ref[...])) * (1.0 / HEAD_DIM)
        return t * lax.rsqrt(ms + EPS) * gain

    q = head_norm(_dot(h, w(2)), qg_ref[...])
    k = head_norm(_dot(h, w(3)), kg_ref[...])
    q_ref[...] = (q * Q_SCALE).astype(BF16)
    k_ref[...] = k.astype(BF16)
    a = _dot(h, w(0))
    b = _dot(h, w(1))
    u_ref[...] = (a * jax.nn.sigmoid(b)).astype(BF16)
    v_ref[...] = _dot(h, w(4)).astype(BF16)


def _mix_in(x2d, gain, w_in_b, q_gain, k_gain):
    n, d = x2d.shape
    lane_head = jnp.arange(d) // HEAD_DIM
    head_sum = (lane_head[:, None] == jnp.arange(LANES)[None, :]).astype(BF16)
    head_spread = head_sum.T
    row = pl.BlockSpec((TOKEN_TILE, d), lambda i: (i, 0))
    out = jax.ShapeDtypeStruct((n, d), BF16)
    return pl.pallas_call(
        _mix_in_kernel,
        grid=(n // TOKEN_TILE,),
        in_specs=[row, _const_spec((1, d)), _const_spec((d, 5 * d)), _const_spec((d, LANES)),
                  _const_spec((LANES, d)), _const_spec((1, d)), _const_spec((1, d))],
        out_specs=[row, row, row, row],
        out_shape=[out, out, out, out],
        compiler_params=_params("parallel"),
        name="mix_in",
    )(x2d, gain.reshape(1, d), w_in_b, head_sum, head_spread,
      jnp.tile(q_gain, N_HEADS).reshape(1, d), jnp.tile(k_gain, N_HEADS).reshape(1, d))


def _conv_kernel(u_ref, halo_ref, w_ref, b_ref, o_ref, sh_ref):
    rows = u_ref.shape[1]
    strips = u_ref.shape[2] // LANES
    total = HALO + rows
    first = pl.program_id(1) == 0
    halo = halo_ref[0].astype(F32)
    halo = jnp.where(first, jnp.zeros_like(halo), halo)
    u = u_ref[0].astype(F32)
    for t in range(strips):
        lanes = slice(t * LANES, (t + 1) * LANES)
        sh_ref[0, t, 0:HALO, :] = halo[:, lanes]
        sh_ref[0, t, HALO:total, :] = u[:, lanes]
        for r in range(1, SUBLANES):
            sh_ref[r, t, 0:total - r, :] = sh_ref[0, t, r:total, :]
    lead = HALO - (CONV_WIDTH - 1)
    groups = CONV_ROWS // SUBLANES

    def body(idx, carry):
        t = idx % strips
        base = pl.multiple_of((idx // strips) * CONV_ROWS, CONV_ROWS)
        accs = [b_ref[t]] * groups
        for shift in range(SUBLANES):
            taps = range((shift - lead) % SUBLANES, CONV_WIDTH, SUBLANES)
            ws = {j: w_ref[j, t] for j in taps}
            offs = sorted({lead + j - shift + g * SUBLANES for j in taps for g in range(groups)})
            for off in offs:
                rows8 = sh_ref[shift, t, pl.ds(base + off, SUBLANES), :]
                for j in taps:
                    g, rem = divmod(off - (lead + j - shift), SUBLANES)
                    if rem == 0 and 0 <= g < groups:
                        accs[g] = accs[g] + rows8 * ws[j]
        o_ref[0, t, pl.ds(base, CONV_ROWS), :] = jnp.concatenate(accs, axis=0)
        return carry

    lax.fori_loop(0, (rows // CONV_ROWS) * strips, body, 0)


def _conv(u3d, w, bias):
    b, s, c = u3d.shape
    strips = c // LANES
    per = TOKEN_TILE // HALO
    tile = pl.BlockSpec((1, TOKEN_TILE, c), lambda bi, i: (bi, i, 0))
    halo = pl.BlockSpec((1, HALO, c), lambda bi, i: (bi, jnp.maximum(i * per - 1, 0), 0))
    rep = lambda p: jnp.broadcast_to(p.reshape(-1, strips, 1, LANES),
                                     (p.size // c, strips, SUBLANES, LANES))
    tiles = s // TOKEN_TILE
    out_tile = pl.BlockSpec((1, strips, TOKEN_TILE, LANES), lambda bi, i: (bi * tiles + i, 0, 0, 0))
    return pl.pallas_call(
        _conv_kernel,
        grid=(b, tiles),
        in_specs=[tile, halo, _const_spec((CONV_WIDTH, strips, SUBLANES, LANES)),
                  _const_spec((strips, SUBLANES, LANES))],
        out_specs=out_tile,
        out_shape=jax.ShapeDtypeStruct((b * tiles, strips, TOKEN_TILE, LANES), F32),
        scratch_shapes=[pltpu.VMEM((SUBLANES, strips, HALO + TOKEN_TILE, LANES), F32)],
        compiler_params=_params("parallel", "parallel"),
        name="conv",
    )(u3d, u3d, rep(w), rep(bias)[0])


def _attn_kernel(q_ref, k_ref, v_ref, o_ref, kaug_ref, vaug_ref, kmean_ref, qaug_ref,
                 sa_ref, sb_ref, ra_ref, rb_ref, m_ref, acc_ref):
    a = pl.program_id(2)
    s_len = k_ref.shape[1]
    nb = s_len // MOBA_BLOCK
    blk = MOBA_BLOCK
    span = ATTN_QBLOCKS * blk
    rows = 2 * span

    nt = (((1,), (1,)), ((), ()))

    def build_qaug(step):
        qp = q_ref[0, pl.ds(pl.multiple_of(step * span, span), span), :]
        q2 = jnp.concatenate([qp, qp], axis=0)
        row_head = lax.broadcasted_iota(jnp.int32, (rows, LANES), 0) // span
        lane_head = lax.broadcasted_iota(jnp.int32, (rows, LANES), 1) // HEAD_DIM
        qe = jnp.where(row_head == lane_head, q2, jnp.zeros_like(q2))
        cand = lax.broadcasted_iota(jnp.int32, (nb, rows), 0)
        q_blk = step * ATTN_QBLOCKS + (lax.broadcasted_iota(jnp.int32, (nb, rows), 1) % span) // blk
        gate = lax.dot_general(kmean_ref[...], qe, nt, preferred_element_type=F32)
        gate = jnp.where(cand < q_blk, gate, -jnp.inf)
        sel = cand == q_blk
        for _ in range(MOBA_TOPK):
            mx = jnp.max(gate, axis=0, keepdims=True)
            first = jnp.min(jnp.where(gate == mx, cand, nb), axis=0, keepdims=True)
            pick = (cand == first) & (mx > -jnp.inf)
            sel = sel | pick
            gate = jnp.where(pick, -jnp.inf, gate)
        bias_t = jnp.where(sel, 0.0, NEG_INF).astype(F32)
        bias_t = jnp.concatenate([bias_t, jnp.zeros((LANES - nb, rows), F32)], axis=0)
        return jnp.concatenate([qe, bias_t.T.astype(BF16)], axis=1)

    @pl.when(a == 0)
    def _():
        k = k_ref[0]
        kaug_ref[:, 0:LANES] = k
        row_blk = lax.broadcasted_iota(jnp.int32, (s_len, LANES), 0) // blk
        lane = lax.broadcasted_iota(jnp.int32, (s_len, LANES), 1)
        kaug_ref[:, LANES:2 * LANES] = (row_blk == lane).astype(BF16)
        vaug_ref[:, 0:LANES] = v_ref[0]
        vaug_ref[:, LANES:2 * LANES] = jnp.ones((s_len, LANES), BF16)
        kmean_ref[...] = jnp.mean(k.astype(F32).reshape(nb, blk, LANES), axis=1).astype(BF16)

        def gate_step(step, carry):
            qaug_ref[step] = build_qaug(step)
            return carry

        lax.fori_loop(0, s_len // span, gate_step, 0, unroll=GATE_UNROLL)

    def issue_scores(n, s_ref, rmax_ref):
        start = pl.multiple_of(n * span, span)
        s = lax.dot_general(qaug_ref[a], kaug_ref[pl.ds(start, span), :], nt,
                            preferred_element_type=F32)
        s_ref[...] = s
        rmax_ref[...] = jnp.broadcast_to(jnp.max(s, axis=1, keepdims=True), rmax_ref.shape)

    def absorb(n, s_ref, rmax_ref, own):
        s = s_ref[...]
        if own:
            q_pos = lax.broadcasted_iota(jnp.int32, (rows, span), 0) % span
            k_pos = lax.broadcasted_iota(jnp.int32, (rows, span), 1)
            s = jnp.where(k_pos <= q_pos, s, NEG_INF)
            rmax = jnp.broadcast_to(jnp.max(s, axis=1, keepdims=True), m_ref.shape)
        else:
            rmax = rmax_ref[...]
        m_old = m_ref[...]
        m_new = jnp.maximum(m_old, rmax)
        alpha = jnp.exp2(m_old - m_new)
        p = jnp.exp2(s - jnp.concatenate([m_new] * (span // LANES), axis=1))
        start = pl.multiple_of(n * span, span)
        acc_ref[...] = (jnp.concatenate([alpha, alpha], axis=1) * acc_ref[...]
                        + _dot(p.astype(BF16), vaug_ref[pl.ds(start, span), :]))
        m_ref[...] = m_new

    m_ref[...] = jnp.full(m_ref.shape, 0.1 * NEG_INF, F32)
    acc_ref[...] = jnp.zeros_like(acc_ref)
    bufs = ((sa_ref, ra_ref), (sb_ref, rb_ref))
    issue_scores(0, *bufs[0])

    def run_past(first, count):
        for i in range(count):
            issue_scores(first + i + 1, *bufs[(i + 1) % 2])
            absorb(first + i, *bufs[i % 2], False)

    def unrolled(t, carry):
        run_past(ATTN_UNROLL * t, ATTN_UNROLL)
        return carry

    lax.fori_loop(0, a // ATTN_UNROLL, unrolled, 0)
    done = (a // ATTN_UNROLL) * ATTN_UNROLL
    for left in range(ATTN_UNROLL):
        @pl.when(a - done == left)
        def _(left=left):
            run_past(done, left)
            absorb(a, *bufs[left % 2], True)

    out = acc_ref[:, 0:LANES] / acc_ref[:, LANES:2 * LANES]
    lane_q = lax.broadcasted_iota(jnp.int32, (span, LANES), 1)
    o_ref[0] = jnp.where(lane_q < HEAD_DIM, out[0:span], out[span:rows]).astype(BF16)


def _attn(q, k, v):
    b, s, d = q.shape
    pairs = d // LANES
    nb = s // MOBA_BLOCK
    span = ATTN_QBLOCKS * MOBA_BLOCK
    rows = 2 * span
    steps = s // span
    ospec = pl.BlockSpec((1, span, LANES), lambda bi, hp, i: (bi, i, hp))
    seq = pl.BlockSpec((1, s, LANES), lambda bi, hp, i: (bi, 0, hp))
    return pl.pallas_call(
        _attn_kernel,
        grid=(b, pairs, steps),
        in_specs=[seq, seq, seq],
        out_specs=ospec,
        out_shape=jax.ShapeDtypeStruct((b, s, d), BF16),
        scratch_shapes=[pltpu.VMEM((s, 2 * LANES), BF16), pltpu.VMEM((s, 2 * LANES), BF16),
                        pltpu.VMEM((nb, LANES), BF16),
                        pltpu.VMEM((steps, rows, 2 * LANES), BF16),
                        pltpu.VMEM((rows, span), F32), pltpu.VMEM((rows, span), F32),
                        pltpu.VMEM((rows, LANES), F32), pltpu.VMEM((rows, LANES), F32),
                        pltpu.VMEM((rows, LANES), F32), pltpu.VMEM((rows, 2 * LANES), F32)],
        compiler_params=_params("parallel", "parallel", "arbitrary"),
        name="moba_attn",
    )(q, k, v)


def _merge_kernel(x_ref, cv_ref, at_ref, g_ref, lg_ref, lb_ref, wga_ref, wgb_ref, wcp_ref, wap_ref,
                  wo_ref, o_ref):
    x = x_ref[...]
    h = _rms_rows(x, g_ref[...]).astype(BF16)
    conv = jnp.concatenate([cv_ref[0, t] for t in range(cv_ref.shape[1])], axis=1)
    mu = jnp.mean(conv, axis=-1, keepdims=True)
    cen = conv - mu
    var = jnp.mean(cen * cen, axis=-1, keepdims=True)
    y = cen * lax.rsqrt(var + EPS) * lg_ref[...] + lb_ref[...]
    uc = (y * jax.nn.sigmoid(y)).astype(BF16)
    ya = jax.nn.sigmoid(_dot(h, wga_ref[...])) * _dot(uc, wcp_ref[...])
    yb = jax.nn.sigmoid(_dot(h, wgb_ref[...])) * _dot(at_ref[...], wap_ref[...])
    o_ref[...] = x + _dot((ya + yb).astype(BF16), wo_ref[...])


def _merge(x2d, conv, at, gain, ln_gain, ln_bias, w_in_b, w_cp, w_ap, w_o):
    n, d = x2d.shape
    row = pl.BlockSpec((TOKEN_TILE, d), lambda i: (i, 0))
    conv_tile = pl.BlockSpec((1,) + conv.shape[1:], lambda i: (i, 0, 0, 0))
    vec = _const_spec((1, d))
    wspec = _const_spec((d, d))
    gate_cols = lambda j: pl.BlockSpec((d, d), lambda i: (0, j), pipeline_mode=pl.Buffered(1))
    return pl.pallas_call(
        _merge_kernel,
        grid=(n // TOKEN_TILE,),
        in_specs=[row, conv_tile, row, vec, vec, vec, gate_cols(5), gate_cols(6),
                  wspec, wspec, wspec],
        out_specs=row,
        out_shape=jax.ShapeDtypeStruct((n, d), F32),
        compiler_params=_params("parallel"),
        name="merge",
    )(x2d, conv, at, gain.reshape(1, d), ln_gain.reshape(1, d), ln_bias.reshape(1, d),
      w_in_b, w_in_b, w_cp, w_ap, w_o)


def _layer(x, ffn1_norm, ffn1_w_gate, ffn1_w_up, ffn1_w_down, mix_norm, w_in,
           conv_dw, conv_dw_bias, conv_ln_gain, conv_ln_bias, w_conv_proj,
           q_norm, k_norm, w_attn_proj, w_out,
           ffn2_norm, ffn2_w_gate, ffn2_w_up, ffn2_w_down):
    b, s, d = x.shape
    n = b * s
    assert d == N_HEADS * HEAD_DIM and s % TOKEN_TILE == 0
    assert s % (ATTN_QBLOCKS * MOBA_BLOCK) == 0 and s // MOBA_BLOCK <= LANES
    x1 = _ffn(x.reshape(n, d), ffn1_norm, ffn1_w_gate, ffn1_w_up, ffn1_w_down)
    w_in_b = w_in.astype(BF16)
    u, q, k, v = _mix_in(x1, mix_norm, w_in_b, q_norm, k_norm)
    conv = _conv(u.reshape(b, s, d), conv_dw, conv_dw_bias)
    at = _attn(q.reshape(b, s, d), k.reshape(b, s, d), v.reshape(b, s, d))
    x2 = _merge(x1, conv, at.reshape(n, d), mix_norm, conv_ln_gain, conv_ln_bias, w_in_b,
                w_conv_proj.astype(BF16), w_attn_proj.astype(BF16), w_out.astype(BF16))
    x3 = _ffn(x2, ffn2_norm, ffn2_w_gate, ffn2_w_up, ffn2_w_down)
    return x3.reshape(b, s, d)


def kernel(x, ffn1_norm, ffn1_w_gate, ffn1_w_up, ffn1_w_down, mix_norm, w_in, conv_dw, conv_dw_bias, conv_ln_gain, conv_ln_bias, w_conv_proj, q_norm, k_norm, w_attn_proj, w_out, ffn2_norm, ffn2_w_gate, ffn2_w_up, ffn2_w_down):
    for i in range(ffn1_norm.shape[0]):
        x = _layer(
            x, ffn1_norm[i], ffn1_w_gate[i], ffn1_w_up[i], ffn1_w_down[i], mix_norm[i], w_in[i],
            conv_dw[i], conv_dw_bias[i], conv_ln_gain[i], conv_ln_bias[i], w_conv_proj[i],
            q_norm[i], k_norm[i], w_attn_proj[i], w_out[i],
            ffn2_norm[i], ffn2_w_gate[i], ffn2_w_up[i], ffn2_w_down[i])
    return x
```

```python
import math

import jax
import jax.numpy as jnp
from jax import lax
from jax.experimental import pallas as pl
from jax.experimental.pallas import tpu as pltpu

F32 = jnp.float32
BF16 = jnp.bfloat16

N_HEADS = 16
HEAD_DIM = 64
CONV_WIDTH = 31
MOBA_BLOCK = 256
MOBA_TOPK = 3
EPS = 1e-6
NEG_INF = -1e30
Q_SCALE = HEAD_DIM ** -0.5 * math.log2(math.e)

LANES = 128
SUBLANES = 8
HALO = 32
TOKEN_TILE = 512
FFN_TILE = 1024
FF_CHUNK = 256
CONV_ROWS = 128
ATTN_QBLOCKS = 2
GATE_UNROLL = 4
ATTN_UNROLL = 6
VMEM_LIMIT = 56 * 1024 * 1024


def _params(*sem):
    return pltpu.CompilerParams(dimension_semantics=sem, vmem_limit_bytes=VMEM_LIMIT)


def _const_spec(shape):
    nd = len(shape)
    return pl.BlockSpec(shape, lambda *_: (0,) * nd, pipeline_mode=pl.Buffered(1))


def _rms_rows(x, gain):
    ms = jnp.mean(x * x, axis=-1, keepdims=True)
    return x * lax.rsqrt(ms + EPS) * gain


def _dot(a, b):
    return jnp.dot(a, b, preferred_element_type=F32)


def _ffn_kernel(x_ref, g_ref, wg_ref, wu_ref, wd_ref, o_ref):
    x = x_ref[...]
    h = _rms_rows(x, g_ref[...]).astype(BF16)

    acc = jnp.zeros(x.shape, F32)
    for c in range(0, wg_ref.shape[1], FF_CHUNK):
        cols = slice(c, c + FF_CHUNK)
        g = _dot(h, wg_ref[:, cols])
        u = _dot(h, wu_ref[:, cols])
        a = (g * jax.nn.sigmoid(g) * u).astype(BF16)
        acc = acc + _dot(a, wd_ref[cols, :])
    o_ref[...] = x + 0.5 * acc


def _ffn(x2d, gain, w_gate, w_up, w_down):
    n, d = x2d.shape
    assert w_gate.shape[1] % FF_CHUNK == 0
    wg, wu, wd = w_gate.astype(BF16), w_up.astype(BF16), w_down.astype(BF16)
    row = pl.BlockSpec((FFN_TILE, d), lambda i: (i, 0))
    return pl.pallas_call(
        _ffn_kernel,
        grid=(n // FFN_TILE,),
        in_specs=[row, _const_spec((1, d)), _const_spec(wg.shape), _const_spec(wu.shape),
                  _const_spec(wd.shape)],
        out_specs=row,
        out_shape=jax.ShapeDtypeStruct((n, d), F32),
        compiler_params=_params("parallel"),
        name="ffn",
    )(x2d, gain.reshape(1, d), wg, wu, wd)


def _mix_in_kernel(x_ref, g_ref, w_ref, hs_ref, hx_ref, qg_ref, kg_ref, u_ref, q_ref, k_ref, v_ref):
    h = _rms_rows(x_ref[...], g_ref[...]).astype(BF16)
    d = h.shape[1]
    w = lambda j: w_ref[:, j * d:(j + 1) * d]
    def head_norm(t, gain):
        ss = _dot((t * t).astype(BF16), hs_ref[...])
        hi = ss.astype(BF16)
        lo = (ss - hi.astype(F32)).astype(BF16)
        ms = (_dot(hi, hx_ref[...]) + _dot(lo, hx_ref[...])) * (1.0 / HEAD_DIM)
        return t * lax.rsqrt(ms + EPS) * gain

    q = head_norm(_dot(h, w(2)), qg_ref[...])
    k = head_norm(_dot(h, w(3)), kg_ref[...])
    q_ref[...] = (q * Q_SCALE).astype(BF16)
    k_ref[...] = k.astype(BF16)
    a = _dot(h, w(0))
    b = _dot(h, w(1))
    u_ref[...] = (a * jax.nn.sigmoid(b)).astype(BF16)
    v_ref[...] = _dot(h, w(4)).astype(BF16)


def _mix_in(x2d, gain, w_in_b, q_gain, k_gain):
    n, d = x2d.shape
    lane_head = jnp.arange(d) // HEAD_DIM
    head_sum = (lane_head[:, None] == jnp.arange(LANES)[None, :]).astype(BF16)
    head_spread = head_sum.T
    row = pl.BlockSpec((TOKEN_TILE, d), lambda i: (i, 0))
    out = jax.ShapeDtypeStruct((n, d), BF16)
    return pl.pallas_call(
        _mix_in_kernel,
        grid=(n // TOKEN_TILE,),
        in_specs=[row, _const_spec((1, d)), _const_spec((d, 5 * d)), _const_spec((d, LANES)),
                  _const_spec((LANES, d)), _const_spec((1, d)), _const_spec((1, d))],
        out_specs=[row, row, row, row],
        out_shape=[out, out, out, out],
        compiler_params=_params("parallel"),
        name="mix_in",
    )(x2d, gain.reshape(1, d), w_in_b, head_sum, head_spread,
      jnp.tile(q_gain, N_HEADS).reshape(1, d), jnp.tile(k_gain, N_HEADS).reshape(1, d))


def _conv_kernel(u_ref, halo_ref, w_ref, b_ref, o_ref, sh_ref):
    rows = u_ref.shape[1]
    strips = u_ref.shape[2] // LANES
    total = HALO + rows
    first = pl.program_id(1) == 0
    halo = halo_ref[0].astype(F32)
    halo = jnp.where(first, jnp.zeros_like(halo), halo)
    u = u_ref[0].astype(F32)
    for t in range(strips):
        lanes = slice(t * LANES, (t + 1) * LANES)
        sh_ref[0, t, 0:HALO, :] = halo[:, lanes]
        sh_ref[0, t, HALO:total, :] = u[:, lanes]
        for r in range(1, SUBLANES):
            sh_ref[r, t, 0:total - r, :] = sh_ref[0, t, r:total, :]
    lead = HALO - (CONV_WIDTH - 1)
    groups = CONV_ROWS // SUBLANES

    def body(idx, carry):
        t = idx % strips
        base = pl.multiple_of((idx // strips) * CONV_ROWS, CONV_ROWS)
        accs = [b_ref[t]] * groups
        for shift in range(SUBLANES):
            taps = range((shift - lead) % SUBLANES, CONV_WIDTH, SUBLANES)
            ws = {j: w_ref[j, t] for j in taps}
            offs = sorted({lead + j - shift + g * SUBLANES for j in taps for g in range(groups)})
            for off in offs:
                rows8 = sh_ref[shift, t, pl.ds(base + off, SUBLANES), :]
                for j in taps:
                    g, rem = divmod(off - (lead + j - shift), SUBLANES)
                    if rem == 0 and 0 <= g < groups:
                        accs[g] = accs[g] + rows8 * ws[j]
        o_ref[0, t, pl.ds(base, CONV_ROWS), :] = jnp.concatenate(accs, axis=0)
        return carry

    lax.fori_loop(0, (rows // CONV_ROWS) * strips, body, 0)


def _conv(u3d, w, bias):
    b, s, c = u3d.shape
    strips = c // LANES
    per = TOKEN_TILE // HALO
    tile = pl.BlockSpec((1, TOKEN_TILE, c), lambda bi, i: (bi, i, 0))
    halo = pl.BlockSpec((1, HALO, c), lambda bi, i: (bi, jnp.maximum(i * per - 1, 0), 0))
    rep = lambda p: jnp.broadcast_to(p.reshape(-1, strips, 1, LANES),
                                     (p.size // c, strips, SUBLANES, LANES))
    tiles = s // TOKEN_TILE
    out_tile = pl.BlockSpec((1, strips, TOKEN_TILE, LANES), lambda bi, i: (bi * tiles + i, 0, 0, 0))
    return pl.pallas_call(
        _conv_kernel,
        grid=(b, tiles),
        in_specs=[tile, halo, _const_spec((CONV_WIDTH, strips, SUBLANES, LANES)),
                  _const_spec((strips, SUBLANES, LANES))],
        out_specs=out_tile,
        out_shape=jax.ShapeDtypeStruct((b * tiles, strips, TOKEN_TILE, LANES), F32),
        scratch_shapes=[pltpu.VMEM((SUBLANES, strips, HALO + TOKEN_TILE, LANES), F32)],
        compiler_params=_params("parallel", "parallel"),
        name="conv",
    )(u3d, u3d, rep(w), rep(bias)[0])


def _attn_kernel(q_ref, k_ref, v_ref, o_ref, kaug_ref, vaug_ref, kmean_ref, qaug_ref,
                 sa_ref, sb_ref, ra_ref, rb_ref, m_ref, acc_ref):
    a = pl.program_id(2)
    s_len = k_ref.shape[1]
    nb = s_len // MOBA_BLOCK
    blk = MOBA_BLOCK
    span = ATTN_QBLOCKS * blk
    rows = 2 * span

    nt = (((1,), (1,)), ((), ()))

    def build_qaug(step):
        qp = q_ref[0, pl.ds(pl.multiple_of(step * span, span), span), :]
        q2 = jnp.concatenate([qp, qp], axis=0)
        row_head = lax.broadcasted_iota(jnp.int32, (rows, LANES), 0) // span
        lane_head = lax.broadcasted_iota(jnp.int32, (rows, LANES), 1) // HEAD_DIM
        qe = jnp.where(row_head == lane_head, q2, jnp.zeros_like(q2))
        cand = lax.broadcasted_iota(jnp.int32, (nb, rows), 0)
        q_blk = step * ATTN_QBLOCKS + (lax.broadcasted_iota(jnp.int32, (nb, rows), 1) % span) // blk
        gate = lax.dot_general(kmean_ref[...], qe, nt, preferred_element_type=F32)
        gate = jnp.where(cand < q_blk, gate, -jnp.inf)
        sel = cand == q_blk
        for _ in range(MOBA_TOPK):
            mx = jnp.max(gate, axis=0, keepdims=True)
            first = jnp.min(jnp.where(gate == mx, cand, nb), axis=0, keepdims=True)
            pick = (cand == first) & (mx > -jnp.inf)
            sel = sel | pick
            gate = jnp.where(pick, -jnp.inf, gate)
        bias_t = jnp.where(sel, 0.0, NEG_INF).astype(F32)
        bias_t = jnp.concatenate([bias_t, jnp.zeros((LANES - nb, rows), F32)], axis=0)
        return jnp.concatenate([qe, bias_t.T.astype(BF16)], axis=1)

    @pl.when(a == 0)
    def _():
        k = k_ref[0]
        kaug_ref[:, 0:LANES] = k
        row_blk = lax.broadcasted_iota(jnp.int32, (s_len, LANES), 0) // blk
        lane = lax.broadcasted_iota(jnp.int32, (s_len, LANES), 1)
        kaug_ref[:, LANES:2 * LANES] = (row_blk == lane).astype(BF16)
        vaug_ref[:, 0:LANES] = v_ref[0]
        vaug_ref[:, LANES:2 * LANES] = jnp.ones((s_len, LANES), BF16)
        kmean_ref[...] = jnp.mean(k.astype(F32).reshape(nb, blk, LANES), axis=1).astype(BF16)

        def gate_step(step, carry):
            qaug_ref[step] = build_qaug(step)
            return carry

        lax.fori_loop(0, s_len // span, gate_step, 0, unroll=GATE_UNROLL)

    def issue_scores(n, s_ref, rmax_ref):
        start = pl.multiple_of(n * span, span)
        s = lax.dot_general(qaug_ref[a], kaug_ref[pl.ds(start, span), :], nt,
                            preferred_element_type=F32)
        s_ref[...] = s
        rmax_ref[...] = jnp.broadcast_to(jnp.max(s, axis=1, keepdims=True), rmax_ref.shape)

    def absorb(n, s_ref, rmax_ref, own):
        s = s_ref[...]
        if own:
            q_pos = lax.broadcasted_iota(jnp.int32, (rows, span), 0) % span
            k_pos = lax.broadcasted_iota(jnp.int32, (rows, span), 1)
            s = jnp.where(k_pos <= q_pos, s, NEG_INF)
            rmax = jnp.broadcast_to(jnp.max(s, axis=1, keepdims=True), m_ref.shape)
        else:
            rmax = rmax_ref[...]
        m_old = m_ref[...]
        m_new = jnp.maximum(m_old, rmax)
        alpha = jnp.exp2(m_old - m_new)
        p = jnp.exp2(s - jnp.concatenate([m_new] * (span // LANES), axis=1))
        start = pl.multiple_of(n * span, span)
        acc_ref[...] = (jnp.concatenate([alpha, alpha], axis=1) * acc_ref[...]
                        + _dot(p.astype(BF16), vaug_ref[pl.ds(start, span), :]))
        m_ref[...] = m_new

    m_ref[...] = jnp.full(m_ref.shape, 0.1 * NEG_INF, F32)
    acc_ref[...] = jnp.zeros_like(acc_ref)
    bufs = ((sa_ref, ra_ref), (sb_ref, rb_ref))
    issue_scores(0, *bufs[0])

    def run_past(first, count):
        for i in range(count):
            issue_scores(first + i + 1, *bufs[(i + 1) % 2])
            absorb(first + i, *bufs[i % 2], False)

    def unrolled(t, carry):
        run_past(ATTN_UNROLL * t, ATTN_UNROLL)
        return carry

    lax.fori_loop(0, a // ATTN_UNROLL, unrolled, 0)
    done = (a // ATTN_UNROLL) * ATTN_UNROLL
    for left in range(ATTN_UNROLL):
        @pl.when(a - done == left)
        def _(left=left):
            run_past(done, left)
            absorb(a, *bufs[left % 2], True)

    out = acc_ref[:, 0:LANES] / acc_ref[:, LANES:2 * LANES]
    lane_q = lax.broadcasted_iota(jnp.int32, (span, LANES), 1)
    o_ref[0] = jnp.where(lane_q < HEAD_DIM, out[0:span], out[span:rows]).astype(BF16)


def _attn(q, k, v):
    b, s, d = q.shape
    pairs = d // LANES
    nb = s // MOBA_BLOCK
    span = ATTN_QBLOCKS * MOBA_BLOCK
    rows = 2 * span
    steps = s // span
    ospec = pl.BlockSpec((1, span, LANES), lambda bi, hp, i: (bi, i, hp))
    seq = pl.BlockSpec((1, s, LANES), lambda bi, hp, i: (bi, 0, hp))
    return pl.pallas_call(
        _attn_kernel,
        grid=(b, pairs, steps),
        in_specs=[seq, seq, seq],
        out_specs=ospec,
        out_shape=jax.ShapeDtypeStruct((b, s, d), BF16),
        scratch_shapes=[pltpu.VMEM((s, 2 * LANES), BF16), pltpu.VMEM((s, 2 * LANES), BF16),
                        pltpu.VMEM((nb, LANES), BF16),
                        pltpu.VMEM((steps, rows, 2 * LANES), BF16),
                        pltpu.VMEM((rows, span), F32), pltpu.VMEM((rows, span), F32),
                        pltpu.VMEM((rows, LANES), F32), pltpu.VMEM((rows, LANES), F32),
                        pltpu.VMEM((rows, LANES), F32), pltpu.VMEM((rows, 2 * LANES), F32)],
        compiler_params=_params("parallel", "parallel", "arbitrary"),
        name="moba_attn",
    )(q, k, v)


def _merge_kernel(x_ref, cv_ref, at_ref, g_ref, lg_ref, lb_ref, wga_ref, wgb_ref, wcp_ref, wap_ref,
                  wo_ref, o_ref):
    x = x_ref[...]
    h = _rms_rows(x, g_ref[...]).astype(BF16)
    conv = jnp.concatenate([cv_ref[0, t] for t in range(cv_ref.shape[1])], axis=1)
    mu = jnp.mean(conv, axis=-1, keepdims=True)
    cen = conv - mu
    var = jnp.mean(cen * cen, axis=-1, keepdims=True)
    y = cen * lax.rsqrt(var + EPS) * lg_ref[...] + lb_ref[...]
    uc = (y * jax.nn.sigmoid(y)).astype(BF16)
    ya = jax.nn.sigmoid(_dot(h, wga_ref[...])) * _dot(uc, wcp_ref[...])
    yb = jax.nn.sigmoid(_dot(h, wgb_ref[...])) * _dot(at_ref[...], wap_ref[...])
    o_ref[...] = x + _dot((ya + yb).astype(BF16), wo_ref[...])


def _merge(x2d, conv, at, gain, ln_gain, ln_bias, w_in_b, w_cp, w_ap, w_o):
    n, d = x2d.shape
    row = pl.BlockSpec((TOKEN_TILE, d), lambda i: (i, 0))
    conv_tile = pl.BlockSpec((1,) + conv.shape[1:], lambda i: (i, 0, 0, 0))
    vec = _const_spec((1, d))
    wspec = _const_spec((d, d))
    gate_cols = lambda j: pl.BlockSpec((d, d), lambda i: (0, j), pipeline_mode=pl.Buffered(1))
    return pl.pallas_call(
        _merge_kernel,
        grid=(n // TOKEN_TILE,),
        in_specs=[row, conv_tile, row, vec, vec, vec, gate_cols(5), gate_cols(6),
                  wspec, wspec, wspec],
        out_specs=row,
        out_shape=jax.ShapeDtypeStruct((n, d), F32),
        compiler_params=_params("parallel"),
        name="merge",
    )(x2d, conv, at, gain.reshape(1, d), ln_gain.reshape(1, d), ln_bias.reshape(1, d),
      w_in_b, w_in_b, w_cp, w_ap, w_o)


def _layer(x, ffn1_norm, ffn1_w_gate, ffn1_w_up, ffn1_w_down, mix_norm, w_in,
           conv_dw, conv_dw_bias, conv_ln_gain, conv_ln_bias, w_conv_proj,
           q_norm, k_norm, w_attn_proj, w_out,
           ffn2_norm, ffn2_w_gate, ffn2_w_up, ffn2_w_down):
    b, s, d = x.shape
    n = b * s
    assert d == N_HEADS * HEAD_DIM and s % TOKEN_TILE == 0 and n % FFN_TILE == 0
    assert s % (ATTN_QBLOCKS * MOBA_BLOCK) == 0 and s // MOBA_BLOCK <= LANES
    x1 = _ffn(x.reshape(n, d), ffn1_norm, ffn1_w_gate, ffn1_w_up, ffn1_w_down)
    w_in_b = w_in.astype(BF16)
    u, q, k, v = _mix_in(x1, mix_norm, w_in_b, q_norm, k_norm)
    conv = _conv(u.reshape(b, s, d), conv_dw, conv_dw_bias)
    at = _attn(q.reshape(b, s, d), k.reshape(b, s, d), v.reshape(b, s, d))
    x2 = _merge(x1, conv, at.reshape(n, d), mix_norm, conv_ln_gain, conv_ln_bias, w_in_b,
                w_conv_proj.astype(BF16), w_attn_proj.astype(BF16), w_out.astype(BF16))
    x3 = _ffn(x2, ffn2_norm, ffn2_w_gate, ffn2_w_up, ffn2_w_down)
    return x3.reshape(b, s, d)


def kernel(x, ffn1_norm, ffn1_w_gate, ffn1_w_up, ffn1_w_down, mix_norm, w_in, conv_dw, conv_dw_bias, conv_ln_gain, conv_ln_bias, w_conv_proj, q_norm, k_norm, w_attn_proj, w_out, ffn2_norm, ffn2_w_gate, ffn2_w_up, ffn2_w_down):
    for i in range(ffn1_norm.shape[0]):
        x = _layer(
            x, ffn1_norm[i], ffn1_w_gate[i], ffn1_w_up[i], ffn1_w_down[i], mix_norm[i], w_in[i],
            conv_dw[i], conv_dw_bias[i], conv_ln_gain[i], conv_ln_bias[i], w_conv_proj[i],
            q_norm[i], k_norm[i], w_attn_proj[i], w_out[i],
            ffn2_norm[i], ffn2_w_gate[i], ffn2_w_up[i], ffn2_w_down[i])
    return x
```
